```python
import jax
import jax.numpy as jnp
from jax import lax
import numpy as np

D_MODEL = 1024
BATCH = 2
SEQ = 16384
DEPTH = 4

D_MIX = 2 * D_MODEL
SSD_WIDTH = D_MIX // 2
SSD_HEAD_DIM = 64
SSD_HEADS = SSD_WIDTH // SSD_HEAD_DIM
SSD_GROUPS = 2
SSD_STATE = 64
SSD_CONV = 4
SSD_CHUNK = 128
D_XBC = SSD_WIDTH + 2 * SSD_GROUPS * SSD_STATE
D_SSD_IN = SSD_WIDTH + D_XBC + SSD_HEADS
MLSTM_WIDTH = D_MIX // 4
MLSTM_HEAD_DIM = 128
MLSTM_HEADS = MLSTM_WIDTH // MLSTM_HEAD_DIM
MLSTM_CONV = 4
MLSTM_CHUNK = 128
D_MLSTM_IN = 2 * MLSTM_WIDTH + 2 * MLSTM_HEADS
RWKV_WIDTH = D_MIX // 4
RWKV_HEAD_DIM = 64
RWKV_HEADS = RWKV_WIDTH // RWKV_HEAD_DIM
RWKV_DECAY_RANK = 64
RWKV_AAA_RANK = 64
RWKV_GATE_RANK = 128
D_RWKV_IN = 3 * RWKV_WIDTH + RWKV_DECAY_RANK + RWKV_AAA_RANK + RWKV_GATE_RANK
D_IN = D_SSD_IN + D_MLSTM_IN + D_RWKV_IN
D_FF = 2816
FFN_HALF = 0.5
EPS = 1e-6
RWKV_GN_EPS = 64e-5

kernel_name = 'hybrid_ssd_mlstm_rwkv7_macaron'


def rms_norm(x, g):
    xf = x.astype(jnp.float32)
    xf = xf * lax.rsqrt(jnp.mean(xf * xf, axis=-1, keepdims=True) + EPS)
    return (xf * g).astype(x.dtype)


def group_rms_norm(y, g, n_groups):
    b, t, w = y.shape
    yf = y.astype(jnp.float32).reshape(b, t, n_groups, w // n_groups)
    yf = yf * lax.rsqrt(jnp.mean(yf * yf, axis=-1, keepdims=True) + EPS)
    return yf.reshape(b, t, w) * g


def swiglu(x, w_gate_up, w_down):
    gate, up = jnp.split(x @ w_gate_up, 2, axis=-1)
    return (jax.nn.silu(gate) * up) @ w_down


def causal_depthwise_conv(x, w, b):
    k = w.shape[0]
    y = lax.conv_general_dilated(x, w[:, None, :].astype(x.dtype), window_strides=(1,),
                                 padding=[(k - 1, 0)], dimension_numbers=('NWC', 'WIO', 'NWC'),
                                 feature_group_count=x.shape[-1])
    return y + b


def causal_mask(n):
    return jnp.tril(jnp.ones((n, n), dtype=bool))


def ssd_mixer(u, conv_w, conv_b, dt_bias, a_log, d_skip, norm_g):
    bsz, t, _ = u.shape
    nc, L, G, J, P, N = t // SSD_CHUNK, SSD_CHUNK, SSD_GROUPS, SSD_HEADS // SSD_GROUPS, SSD_HEAD_DIM, SSD_STATE
    z, xbc, dt_raw = jnp.split(u, [SSD_WIDTH, SSD_WIDTH + D_XBC], axis=-1)
    xbc = jax.nn.silu(causal_depthwise_conv(xbc, conv_w, conv_b).astype(jnp.float32))
    xs, b_in, c_in = jnp.split(xbc, [SSD_WIDTH, SSD_WIDTH + G * N], axis=-1)
    dt = jax.nn.softplus((dt_raw + dt_bias).astype(jnp.float32)).reshape(bsz, nc, L, G, J)
    a = dt * (-jnp.exp(a_log.astype(jnp.float32))).reshape(G, J)
    a_cum = jnp.cumsum(a, axis=2)
    X = xs.reshape(bsz, nc, L, G, J, P)
    Bc = b_in.reshape(bsz, nc, L, G, N)
    Cc = c_in.reshape(bsz, nc, L, G, N)
    x_dt = X * dt[..., None]
    seg = a_cum[:, :, :, None] - a_cum[:, :, None, :]
    decay_ls = jnp.exp(jnp.where(causal_mask(L)[:, :, None, None], seg, -jnp.inf))
    cb = jnp.einsum('bclgn,bcsgn->bclsg', Cc, Bc)
    y_diag = jnp.einsum('bclsgj,bcsgjp->bclgjp', cb[..., None] * decay_ls, x_dt)
    decay_to_end = jnp.exp(a_cum[:, :, -1:] - a_cum)
    states = jnp.einsum('bcsgn,bcsgjp->bcgjpn', Bc, x_dt * decay_to_end[..., None])
    chunk_decay = jnp.exp(a_cum[:, :, -1])

    def step(s, inp):
        st, dec = inp
        return s * dec[..., None, None] + st, s

    _, s_prev = lax.scan(step, jnp.zeros_like(states[:, 0]),
                         (jnp.moveaxis(states, 1, 0), jnp.moveaxis(chunk_decay, 1, 0)))
    s_prev = jnp.moveaxis(s_prev, 0, 1)
    y_off = jnp.einsum('bclgn,bcgjpn->bclgjp', Cc, s_prev) * jnp.exp(a_cum)[..., None]
    y = y_diag + y_off + X * d_skip.astype(jnp.float32).reshape(G, J, 1)
    y = y.reshape(bsz, t, SSD_WIDTH) * jax.nn.silu(z.astype(jnp.float32))
    return group_rms_norm(y, norm_g, G).astype(u.dtype)


def mlstm_mixer(u, conv_w, conv_b, wq, wk, wv, gate_bias, norm_g):
    bsz, t, _ = u.shape
    H, Dh, L = MLSTM_HEADS, MLSTM_HEAD_DIM, MLSTM_CHUNK
    nc = t // L
    xm, o_pre, if_pre = jnp.split(u, [MLSTM_WIDTH, 2 * MLSTM_WIDTH], axis=-1)
    xc = jax.nn.silu(causal_depthwise_conv(xm, conv_w, conv_b).astype(jnp.float32))
    xc_h = xc.reshape(bsz, t, H, Dh)
    xm_h = xm.astype(jnp.float32).reshape(bsz, t, H, Dh)
    q = jnp.einsum('bthd,hde->bthe', xc_h, wq)
    k = jnp.einsum('bthd,hde->bthe', xc_h, wk) * (Dh ** -0.5)
    v = jnp.einsum('bthd,hde->bthe', xm_h, wv)
    gates = (if_pre + gate_bias).astype(jnp.float32)
    log_i = gates[..., :H].reshape(bsz, nc, L, H)
    log_f = jax.nn.log_sigmoid(gates[..., H:]).reshape(bsz, nc, L, H)
    o = jax.nn.sigmoid(o_pre.astype(jnp.float32)).reshape(bsz, t, H, Dh)
    qc = q.reshape(bsz, nc, L, H, Dh)
    kc = k.reshape(bsz, nc, L, H, Dh)
    vc = v.reshape(bsz, nc, L, H, Dh)
    bcum = jnp.cumsum(log_f, axis=2)
    d_log = bcum[:, :, :, None] - bcum[:, :, None, :] + log_i[:, :, None, :]
    d_log = jnp.where(causal_mask(L)[:, :, None], d_log, -jnp.inf)
    m_intra = jnp.max(d_log, axis=3)
    end_log = bcum[:, :, -1:] - bcum + log_i
    m_loc = jnp.max(end_log, axis=2)
    w_end = jnp.exp(end_log - m_loc[:, :, None])
    c_loc = jnp.einsum('bcshd,bcshe->bchde', vc * w_end[..., None], kc)
    n_loc = jnp.einsum('bcsh,bcshe->bche', w_end, kc)
    b_end = bcum[:, :, -1]

    def step(carry, inp):
        c_st, n_st, m_st = carry
        cl, nl, ml, be = inp
        m_new = jnp.maximum(be + m_st, ml)
        s_old = jnp.exp(be + m_st - m_new)
        s_new = jnp.exp(ml - m_new)
        c_new = c_st * s_old[..., None, None] + cl * s_new[..., None, None]
        n_new = n_st * s_old[..., None] + nl * s_new[..., None]
        return (c_new, n_new, m_new), (c_st, n_st, m_st)

    init = (jnp.zeros((bsz, H, Dh, Dh), jnp.float32), jnp.zeros((bsz, H, Dh), jnp.float32),
            jnp.zeros((bsz, H), jnp.float32))
    _, (c_prev, n_prev, m_prev) = lax.scan(
        step, init, (jnp.moveaxis(c_loc, 1, 0), jnp.moveaxis(n_loc, 1, 0),
                     jnp.moveaxis(m_loc, 1, 0), jnp.moveaxis(b_end, 1, 0)))
    c_prev = jnp.moveaxis(c_prev, 0, 1)
    n_prev = jnp.moveaxis(n_prev, 0, 1)
    m_prev = jnp.moveaxis(m_prev, 0, 1)
    inter_log = bcum + m_prev[:, :, None]
    m_t = jnp.maximum(inter_log, m_intra)
    w_intra = jnp.exp(d_log - m_t[:, :, :, None])
    w_inter = jnp.exp(inter_log - m_t)
    qk = jnp.einsum('bclhe,bcshe->bclsh', qc, kc) * w_intra
    num = (jnp.einsum('bclsh,bcshd->bclhd', qk, vc)
           + jnp.einsum('bclhe,bchde->bclhd', qc, c_prev) * w_inter[..., None])
    den = jnp.sum(qk, axis=3) + jnp.einsum('bclhe,bche->bclh', qc, n_prev) * w_inter
    h = num / jnp.maximum(jnp.abs(den), jnp.exp(-m_t))[..., None]
    h = h.reshape(bsz, t, H, Dh) * o
    return group_rms_norm(h.reshape(bsz, t, MLSTM_WIDTH), norm_g, H).astype(u.dtype)


def rwkv7_mixer(u, shift_mu, w0, w_up, a0, a_up, g_up, k_k, k_a, r_k, ln_w, ln_b):
    bsz, t, _ = u.shape
    H, N, W = RWKV_HEADS, RWKV_HEAD_DIM, RWKV_WIDTH
    uf = u.astype(jnp.float32)
    u_prev = jnp.pad(uf, ((0, 0), (1, 0), (0, 0)))[:, :-1]
    us = uf + (u_prev - uf) * shift_mu
    r, k, v, wl, al, gl = jnp.split(
        us, [W, 2 * W, 3 * W, 3 * W + RWKV_DECAY_RANK, 3 * W + RWKV_DECAY_RANK + RWKV_AAA_RANK], axis=-1)
    w_log = -jax.nn.softplus(-(w0 + jnp.tanh(wl) @ w_up)) - 0.5
    decay = jnp.exp(-jnp.exp(w_log))
    a = jax.nn.sigmoid(a0 + al @ a_up)
    g = jax.nn.sigmoid(gl) @ g_up
    kk = (k * k_k).reshape(bsz, t, H, N)
    kk = kk / jnp.maximum(jnp.sqrt(jnp.sum(kk * kk, axis=-1, keepdims=True)), 1e-6)
    k = k * (1.0 + (a - 1.0) * k_a)
    rh, kh, vh = (z.reshape(bsz, t, H, N) for z in (r, k, v))
    dh, ah = decay.reshape(bsz, t, H, N), a.reshape(bsz, t, H, N)

    def tm(z):
        return jnp.moveaxis(z, 1, 0)

    def step(s, inp):
        r_t, w_t, k_t, v_t, kk_t, a_t = inp
        sa = jnp.einsum('bhvk,bhk->bhv', s, -kk_t)
        s = (s * w_t[:, :, None, :] + sa[..., None] * (kk_t * a_t)[:, :, None, :]
             + v_t[..., None] * k_t[:, :, None, :])
        return s, jnp.einsum('bhvk,bhk->bhv', s, r_t)

    _, y = lax.scan(step, jnp.zeros((bsz, H, N, N), jnp.float32),
                    (tm(rh), tm(dh), tm(kh), tm(vh), tm(kk), tm(ah)))
    y = jnp.moveaxis(y, 0, 1)
    mu = jnp.mean(y, axis=-1, keepdims=True)
    var = jnp.mean(jnp.square(y - mu), axis=-1, keepdims=True)
    y = (y - mu) * lax.rsqrt(var + RWKV_GN_EPS) * ln_w.reshape(H, N) + ln_b.reshape(H, N)
    y = y + jnp.sum(rh * kh * r_k, axis=-1, keepdims=True) * vh
    return (y.reshape(bsz, t, W) * g).astype(u.dtype)


def setup_inputs(seed: int = 0) -> dict:
    key = jax.random.key(seed)
    keys = iter(jax.random.split(key, 48))
    L, D = DEPTH, D_MODEL

    def nrm(shape, scale):
        return scale * jax.random.normal(next(keys), shape, jnp.float32)

    def unif(shape, lo, hi):
        return jax.random.uniform(next(keys), shape, jnp.float32, lo, hi)

    dt_init = jnp.exp(unif((L, SSD_HEADS), float(np.log(1e-3)), float(np.log(1e-1))))
    return {
        'x': nrm((BATCH, SEQ, D), 1.0),
        'ffn1_norm': 1.0 + nrm((L, D), 0.02),
        'ffn1_w_gate_up': nrm((L, D, 2 * D_FF), D ** -0.5),
        'ffn1_w_down': nrm((L, D_FF, D), D_FF ** -0.5),
        'mix_norm': 1.0 + nrm((L, D), 0.02),
        'w_in': nrm((L, D, D_IN), D ** -0.5),
        'ssd_conv_w': nrm((L, SSD_CONV, D_XBC), SSD_CONV ** -0.5),
        'ssd_conv_b': nrm((L, D_XBC), 0.02),
        'ssd_dt_bias': dt_init + jnp.log(-jnp.expm1(-dt_init)),
        'ssd_a_log': jnp.log(unif((L, SSD_HEADS), 1.0, 16.0)),
        'ssd_d': 1.0 + nrm((L, SSD_HEADS), 0.02),
        'ssd_norm': 1.0 + nrm((L, SSD_WIDTH), 0.02),
        'mlstm_conv_w': nrm((L, MLSTM_CONV, MLSTM_WIDTH), MLSTM_CONV ** -0.5),
        'mlstm_conv_b': nrm((L, MLSTM_WIDTH), 0.02),
        'mlstm_wq': nrm((L, MLSTM_HEADS, MLSTM_HEAD_DIM, MLSTM_HEAD_DIM), MLSTM_HEAD_DIM ** -0.5),
        'mlstm_wk': nrm((L, MLSTM_HEADS, MLSTM_HEAD_DIM, MLSTM_HEAD_DIM), MLSTM_HEAD_DIM ** -0.5),
        'mlstm_wv': nrm((L, MLSTM_HEADS, MLSTM_HEAD_DIM, MLSTM_HEAD_DIM), MLSTM_HEAD_DIM ** -0.5),
        'mlstm_gate_bias': jnp.concatenate(
            [nrm((L, MLSTM_HEADS), 0.1),
             jnp.linspace(3.0, 6.0, MLSTM_HEADS, dtype=jnp.float32) + nrm((L, MLSTM_HEADS), 0.1)], axis=-1),
        'mlstm_norm': 1.0 + nrm((L, MLSTM_WIDTH), 0.02),
        'rwkv_shift_mu': unif((L, D_RWKV_IN), 0.0, 1.0),
        'rwkv_w0': jnp.linspace(-6.0, -1.0, RWKV_WIDTH, dtype=jnp.float32) + nrm((L, RWKV_WIDTH), 0.1),
        'rwkv_w_up': nrm((L, RWKV_DECAY_RANK, RWKV_WIDTH), 0.5 * RWKV_DECAY_RANK ** -0.5),
        'rwkv_a0': nrm((L, RWKV_WIDTH), 0.1),
        'rwkv_a_up': nrm((L, RWKV_AAA_RANK, RWKV_WIDTH), 0.5 * RWKV_AAA_RANK ** -0.5),
        'rwkv_g_up': nrm((L, RWKV_GATE_RANK, RWKV_WIDTH), RWKV_GATE_RANK ** -0.5),
        'rwkv_k_k': 0.85 + nrm((L, RWKV_WIDTH), 0.02),
        'rwkv_k_a': 1.0 + nrm((L, RWKV_WIDTH), 0.02),
        'rwkv_r_k': nrm((L, RWKV_HEADS, RWKV_HEAD_DIM), 0.1),
        'rwkv_ln_w': 1.0 + nrm((L, RWKV_WIDTH), 0.02),
        'rwkv_ln_b': nrm((L, RWKV_WIDTH), 0.02),
        'w_out': nrm((L, D_MIX, D), D_MIX ** -0.5),
        'ffn2_norm': 1.0 + nrm((L, D), 0.02),
        'ffn2_w_gate_up': nrm((L, D, 2 * D_FF), D ** -0.5),
        'ffn2_w_down': nrm((L, D_FF, D), D_FF ** -0.5),
        'final_norm': 1.0 + nrm((D,), 0.02),
    }


def reference(x, ffn1_norm, ffn1_w_gate_up, ffn1_w_down, mix_norm, w_in,
              ssd_conv_w, ssd_conv_b, ssd_dt_bias, ssd_a_log, ssd_d, ssd_norm,
              mlstm_conv_w, mlstm_conv_b, mlstm_wq, mlstm_wk, mlstm_wv, mlstm_gate_bias, mlstm_norm,
              rwkv_shift_mu, rwkv_w0, rwkv_w_up, rwkv_a0, rwkv_a_up, rwkv_g_up, rwkv_k_k, rwkv_k_a,
              rwkv_r_k, rwkv_ln_w, rwkv_ln_b, w_out, ffn2_norm, ffn2_w_gate_up, ffn2_w_down, final_norm):
    for l in range(DEPTH):
        x = x + FFN_HALF * swiglu(rms_norm(x, ffn1_norm[l]), ffn1_w_gate_up[l], ffn1_w_down[l])
        u = rms_norm(x, mix_norm[l]) @ w_in[l]
        u_ssd, u_ml, u_rw = jnp.split(u, [D_SSD_IN, D_SSD_IN + D_MLSTM_IN], axis=-1)
        y_ssd = ssd_mixer(u_ssd, ssd_conv_w[l], ssd_conv_b[l], ssd_dt_bias[l], ssd_a_log[l],
                          ssd_d[l], ssd_norm[l])
        y_ml = mlstm_mixer(u_ml, mlstm_conv_w[l], mlstm_conv_b[l], mlstm_wq[l], mlstm_wk[l],
                           mlstm_wv[l], mlstm_gate_bias[l], mlstm_norm[l])
        y_rw = rwkv7_mixer(u_rw, rwkv_shift_mu[l], rwkv_w0[l], rwkv_w_up[l], rwkv_a0[l], rwkv_a_up[l],
                           rwkv_g_up[l], rwkv_k_k[l], rwkv_k_a[l], rwkv_r_k[l], rwkv_ln_w[l], rwkv_ln_b[l])
        y = jnp.concatenate([y_ssd, y_ml, y_rw], axis=-1)
        x = x + y @ w_out[l]
        x = x + FFN_HALF * swiglu(rms_norm(x, ffn2_norm[l]), ffn2_w_gate_up[l], ffn2_w_down[l])
    return rms_norm(x, final_norm)
```

```python
import functools

import jax
import jax.numpy as jnp
from jax import lax
from jax.experimental import pallas as pl
from jax.experimental.pallas import tpu as pltpu

D_MODEL = 1024
D_FF = 2816
SSD_WIDTH = 1024
SSD_HEAD_DIM = 64
SSD_HEADS = 16
SSD_GROUPS = 2
SSD_STATE = 64
SSD_CHUNK = 128
D_XBC = SSD_WIDTH + 2 * SSD_GROUPS * SSD_STATE
MLSTM_WIDTH = 512
MLSTM_HEADS = 4
MLSTM_HEAD_DIM = 128
MLSTM_CHUNK = 128
RWKV_WIDTH = 512
RWKV_HEADS = 8
RWKV_HEAD_DIM = 64
RWKV_CHUNK = 64
CONV_K = 4
EPS = 1e-6
RWKV_GN_EPS = 64e-5

D_SSD_IN = SSD_WIDTH + D_XBC + SSD_HEADS
D_MLSTM_IN = 2 * MLSTM_WIDTH + 2 * MLSTM_HEADS
D_RWKV_IN = 3 * RWKV_WIDTH + 64 + 64 + 128

U_SSD_W = SSD_WIDTH + D_XBC
U_ML_W = 2 * MLSTM_WIDTH
U_RW_W = D_RWKV_IN
U_SM_W = 128
SM_DT = 0
SM_I = SSD_HEADS
SM_F = SSD_HEADS + MLSTM_HEADS

HALO = 8
V7X_VMEM_LIMIT = 56 * 1024 * 1024
HIGHEST = lax.Precision.HIGHEST


def _silu(x):
    return x * jax.nn.sigmoid(x)


def _softplus(x):
    return jnp.maximum(x, 0.0) + jnp.log1p(jnp.exp(-jnp.abs(x)))


def _log_sigmoid(x):
    return -_softplus(-x)


def _rms(x, g):
    return x * lax.rsqrt(jnp.mean(x * x, axis=-1, keepdims=True) + EPS) * g


def _dot(a, b):
    return jnp.dot(a, b, preferred_element_type=jnp.float32)


def _dot_hi(a, b):
    return jnp.dot(a, b, preferred_element_type=jnp.float32, precision=HIGHEST)


def _dot_nt(a, b):
    return lax.dot_general(a, b, (((1,), (1,)), ((), ())), preferred_element_type=jnp.float32)


def _dot_tn(a, b):
    return lax.dot_general(a, b, (((0,), (0,)), ((), ())), preferred_element_type=jnp.float32)


def _bf(x):
    return x.astype(jnp.bfloat16)


def _params(*sem):
    return pltpu.CompilerParams(dimension_semantics=sem, vmem_limit_bytes=V7X_VMEM_LIMIT)


FFN_TM = 1024
FFN_TF = 256


def _ffn_kernel(x_ref, g_ref, wg_ref, wu_ref, wd_ref, fg_ref, o_ref, h_ref, acc_ref, *, final_norm):
    f = pl.program_id(1)

    @pl.when(f == 0)
    def _():
        h_ref[...] = _bf(_rms(x_ref[...], g_ref[...]))
        acc_ref[...] = jnp.zeros_like(acc_ref)

    h = h_ref[...]
    gate = _dot(h, wg_ref[...])
    up = _dot(h, wu_ref[...])
    acc_ref[...] += _dot(_bf(_silu(gate) * up), wd_ref[...])

    @pl.when(f == pl.num_programs(1) - 1)
    def _():
        y = x_ref[...] + 0.5 * acc_ref[...]
        if final_norm:
            y = _rms(y, fg_ref[...])
        o_ref[...] = y


def _ffn(x2, g, w_gate_up, w_down, final_g, final_norm):
    n = x2.shape[0]
    nf = D_FF // FFN_TF
    return pl.pallas_call(
        functools.partial(_ffn_kernel, final_norm=final_norm),
        grid=(n // FFN_TM, nf),
        in_specs=[
            pl.BlockSpec((FFN_TM, D_MODEL), lambda i, f: (i, 0)),
            pl.BlockSpec((1, D_MODEL), lambda i, f: (0, 0)),
            pl.BlockSpec((D_MODEL, FFN_TF), lambda i, f: (0, f)),
            pl.BlockSpec((D_MODEL, FFN_TF), lambda i, f: (0, f + nf)),
            pl.BlockSpec((FFN_TF, D_MODEL), lambda i, f: (f, 0)),
            pl.BlockSpec((1, D_MODEL), lambda i, f: (0, 0)),
        ],
        out_specs=pl.BlockSpec((FFN_TM, D_MODEL), lambda i, f: (i, 0)),
        out_shape=jax.ShapeDtypeStruct((n, D_MODEL), jnp.float32),
        scratch_shapes=[pltpu.VMEM((FFN_TM, D_MODEL), jnp.bfloat16),
                        pltpu.VMEM((FFN_TM, D_MODEL), jnp.float32)],
        compiler_params=_params("parallel", "arbitrary"),
        name="ffn",
    )(x2, g, w_gate_up, w_gate_up, w_down, final_g)


INP_TM = 256


def _inproj_kernel(x_ref, g_ref, wssd_ref, wml_ref, wrw_ref, wsm_ref, ossd_ref, oml_ref, orw_ref, osm_ref):
    h = _bf(_rms(x_ref[...], g_ref[...]))
    ossd_ref[...] = _dot(h, wssd_ref[...])
    oml_ref[...] = _dot(h, wml_ref[...])
    orw_ref[...] = _dot(h, wrw_ref[...])
    osm_ref[...] = _dot(h, wsm_ref[...])


def _inproj(x2, g, w_ssd, w_ml, w_rw, w_sm):
    n = x2.shape[0]
    widths = (U_SSD_W, U_ML_W, U_RW_W, U_SM_W)
    return pl.pallas_call(
        _inproj_kernel,
        grid=(n // INP_TM,),
        in_specs=[pl.BlockSpec((INP_TM, D_MODEL), lambda i: (i, 0)),
                  pl.BlockSpec((1, D_MODEL), lambda i: (0, 0))]
                 + [pl.BlockSpec((D_MODEL, w), lambda i: (0, 0)) for w in widths],
        out_specs=[pl.BlockSpec((INP_TM, w), lambda i: (i, 0)) for w in widths],
        out_shape=[jax.ShapeDtypeStruct((n, w), jnp.float32) for w in widths],
        compiler_params=_params("parallel"),
        name="inproj",
    )(x2, g, w_ssd, w_ml, w_rw, w_sm)


def _causal_conv(ext_ref, cur, w_ref, b_ref, first):
    rows = cur.shape[0]

    @pl.when(first)
    def _():
        ext_ref[0:HALO, :] = jnp.zeros((HALO, cur.shape[1]), jnp.float32)

    ext_ref[HALO:HALO + rows, :] = cur
    acc = b_ref[...] + cur * w_ref[CONV_K - 1:CONV_K, :]
    for j in range(CONV_K - 1):
        acc = acc + ext_ref[pl.ds(HALO - (CONV_K - 1) + j, rows), :] * w_ref[j:j + 1, :]
    ext_ref[0:HALO, :] = cur[rows - HALO:rows, :]
    return acc


def _ssd_kernel(u_ref, sm_ref, cw_ref, cb_ref, dtb_ref, nega_ref, dskip_ref, ng_ref, tril_ref,
                y_ref, ext_ref, st_ref):
    L, P, N, G = SSD_CHUNK, SSD_HEAD_DIM, SSD_STATE, SSD_GROUPS
    J = SSD_HEADS // G
    first = pl.program_id(1) == 0

    @pl.when(first)
    def _():
        st_ref[...] = jnp.zeros_like(st_ref)

    z = u_ref[0, :, 0:SSD_WIDTH]
    xbc = _silu(_causal_conv(ext_ref, u_ref[0, :, SSD_WIDTH:U_SSD_W], cw_ref, cb_ref, first))
    xs = xbc[:, 0:SSD_WIDTH]
    dt = _softplus(sm_ref[0] + dtb_ref[...])
    a = dt * nega_ref[...]
    a_cum = _dot_hi(tril_ref[...], a)
    a_cum_t = a_cum.T
    exp_a = jnp.exp(a_cum)
    dte = jnp.exp(a_cum[L - 1:L, :] - a_cum)
    row = lax.broadcasted_iota(jnp.int32, (L, L), 0)
    col = lax.broadcasted_iota(jnp.int32, (L, L), 1)
    causal = row >= col

    for g in range(G):
        b_g = xbc[:, SSD_WIDTH + g * N:SSD_WIDTH + (g + 1) * N]
        c_g = xbc[:, SSD_WIDTH + G * N + g * N:SSD_WIDTH + G * N + (g + 1) * N]
        cb = _dot_nt(_bf(c_g), _bf(b_g))
        s_prev = st_ref[g]
        y_off = _dot(_bf(c_g), _bf(s_prev))
        xdd_parts = []
        for j in range(J):
            h = g * J + j
            x_h = xs[:, h * P:(h + 1) * P]
            x_dt = x_h * dt[:, h:h + 1]
            seg = a_cum[:, h:h + 1] - a_cum_t[h:h + 1, :]
            m = cb * jnp.exp(jnp.where(causal, seg, -jnp.inf))
            y_h = (_dot(_bf(m), _bf(x_dt)) + y_off[:, j * P:(j + 1) * P] * exp_a[:, h:h + 1]
                   + x_h * dskip_ref[:, h * P:(h + 1) * P])
            y_ref[0, :, h * P:(h + 1) * P] = y_h
            xdd_parts.append(x_dt * dte[:, h:h + 1])
        xdd = jnp.concatenate(xdd_parts, axis=1)
        new_st = _dot_tn(_bf(b_g), _bf(xdd))
        cd = jnp.concatenate(
            [jnp.broadcast_to(exp_a[L - 1:L, g * J + j:g * J + j + 1], (1, P)) for j in range(J)], axis=1)
        st_ref[g] = s_prev * cd + new_st

    y = y_ref[0] * _silu(z)
    gw = SSD_WIDTH // G
    for g in range(G):
        yg = y[:, g * gw:(g + 1) * gw]
        y_ref[0, :, g * gw:(g + 1) * gw] = _rms(yg, ng_ref[:, g * gw:(g + 1) * gw])


def _ssd(u_ssd, u_sm, conv_w, conv_b, dtb_pad, nega_pad, dskip_wide, norm_g, tril):
    bsz, t, _ = u_ssd.shape
    nc = t // SSD_CHUNK
    const = lambda b, c: (0, 0)
    return pl.pallas_call(
        _ssd_kernel,
        grid=(bsz, nc),
        in_specs=[
            pl.BlockSpec((1, SSD_CHUNK, U_SSD_W), lambda b, c: (b, c, 0)),
            pl.BlockSpec((1, SSD_CHUNK, U_SM_W), lambda b, c: (b, c, 0)),
            pl.BlockSpec((CONV_K, D_XBC), const),
            pl.BlockSpec((1, D_XBC), const),
            pl.BlockSpec((1, U_SM_W), const),
            pl.BlockSpec((1, U_SM_W), const),
            pl.BlockSpec((1, SSD_WIDTH), const),
            pl.BlockSpec((1, SSD_WIDTH), const),
            pl.BlockSpec((SSD_CHUNK, SSD_CHUNK), const),
        ],
        out_specs=pl.BlockSpec((1, SSD_CHUNK, SSD_WIDTH), lambda b, c: (b, c, 0)),
        out_shape=jax.ShapeDtypeStruct((bsz, t, SSD_WIDTH), jnp.float32),
        scratch_shapes=[pltpu.VMEM((HALO + SSD_CHUNK, D_XBC), jnp.float32),
                        pltpu.VMEM((SSD_GROUPS, SSD_STATE, SSD_WIDTH // SSD_GROUPS), jnp.float32)],
        compiler_params=_params("parallel", "arbitrary"),
        name="ssd",
    )(u_ssd, u_sm, conv_w, conv_b, dtb_pad, nega_pad, dskip_wide, norm_g, tril)


def _mlstm_kernel(u_ref, sm_ref, cw_ref, cb_ref, wq_ref, wk_ref, wv_ref, gb_ref, ng_ref, tril_ref,
                  y_ref, ext_ref, c_ref, n_ref, m_ref):
    L, H, Dh = MLSTM_CHUNK, MLSTM_HEADS, MLSTM_HEAD_DIM
    first = pl.program_id(1) == 0

    @pl.when(first)
    def _():
        c_ref[...] = jnp.zeros_like(c_ref)
        n_ref[...] = jnp.zeros_like(n_ref)
        m_ref[...] = jnp.zeros_like(m_ref)

    xm = u_ref[0, :, 0:MLSTM_WIDTH]
    o_pre = u_ref[0, :, MLSTM_WIDTH:U_ML_W]
    xc = _silu(_causal_conv(ext_ref, xm, cw_ref, cb_ref, first))
    gates = sm_ref[0] + gb_ref[...]
    log_f = _log_sigmoid(gates)
    bcum = _dot_hi(tril_ref[...], log_f)
    bcum_t = bcum.T
    gates_t = gates.T
    row = lax.broadcasted_iota(jnp.int32, (L, L), 0)
    col = lax.broadcasted_iota(jnp.int32, (L, L), 1)
    causal = row >= col

    for h in range(H):
        sl = slice(h * Dh, (h + 1) * Dh)
        xc_h = _bf(xc[:, sl])
        q = _dot(xc_h, wq_ref[h])
        k = _dot(xc_h, wk_ref[h]) * (Dh ** -0.5)
        v = _dot(_bf(xm[:, sl]), wv_ref[h])
        b_col = bcum[:, SM_F + h:SM_F + h + 1]
        b_row = bcum_t[SM_F + h:SM_F + h + 1, :]
        i_col = gates[:, SM_I + h:SM_I + h + 1]
        i_row = gates_t[SM_I + h:SM_I + h + 1, :]
        b_end = bcum[L - 1:L, SM_F + h:SM_F + h + 1]
        m_st = m_ref[h:h + 1, 0:1]
        n_st = n_ref[h:h + 1, :]
        c_st = c_ref[h]

        d_log = jnp.where(causal, b_col - b_row + i_row, -jnp.inf)
        m_intra = jnp.max(d_log, axis=1, keepdims=True)
        inter_log = b_col + m_st
        m_t = jnp.maximum(inter_log, m_intra)
        w_intra = jnp.exp(d_log - m_t)
        w_inter = jnp.exp(inter_log - m_t)
        qb, kb, vb = _bf(q), _bf(k), _bf(v)
        qk = _dot_nt(qb, kb) * w_intra
        num = _dot(_bf(qk), vb) + _dot(qb, _bf(c_st)) * w_inter
        den = (jnp.sum(qk, axis=1, keepdims=True)
               + jnp.sum(q * n_st, axis=1, keepdims=True) * w_inter)
        hh = num / jnp.maximum(jnp.abs(den), jnp.exp(-m_t))
        hh = hh * jax.nn.sigmoid(o_pre[:, sl])
        y_ref[0, :, sl] = _rms(hh, ng_ref[:, sl])

        end_log = b_end - b_col + i_col
        m_loc = jnp.max(end_log, axis=0, keepdims=True)
        kw = k * jnp.exp(end_log - m_loc)
        c_loc = _dot_tn(_bf(kw), vb)
        n_loc = jnp.sum(kw, axis=0, keepdims=True)
        m_new = jnp.maximum(b_end + m_st, m_loc)
        s_old = jnp.exp(b_end + m_st - m_new)
        s_new = jnp.exp(m_loc - m_new)
        c_ref[h] = c_st * s_old + c_loc * s_new
        n_ref[h:h + 1, :] = n_st * s_old + n_loc * s_new
        m_ref[h:h + 1, :] = jnp.broadcast_to(m_new, (1, 128))


def _mlstm(u_ml, u_sm, conv_w, conv_b, wq, wk, wv, gb_pad, norm_g, tril):
    bsz, t, _ = u_ml.shape
    nc = t // MLSTM_CHUNK
    const2 = lambda b, c: (0, 0)
    const3 = lambda b, c: (0, 0, 0)
    hd = (MLSTM_HEADS, MLSTM_HEAD_DIM, MLSTM_HEAD_DIM)
    return pl.pallas_call(
        _mlstm_kernel,
        grid=(bsz, nc),
        in_specs=[
            pl.BlockSpec((1, MLSTM_CHUNK, U_ML_W), lambda b, c: (b, c, 0)),
            pl.BlockSpec((1, MLSTM_CHUNK, U_SM_W), lambda b, c: (b, c, 0)),
            pl.BlockSpec((CONV_K, MLSTM_WIDTH), const2),
            pl.BlockSpec((1, MLSTM_WIDTH), const2),
            pl.BlockSpec(hd, const3),
            pl.BlockSpec(hd, const3),
            pl.BlockSpec(hd, const3),
            pl.BlockSpec((1, U_SM_W), const2),
            pl.BlockSpec((1, MLSTM_WIDTH), const2),
            pl.BlockSpec((MLSTM_CHUNK, MLSTM_CHUNK), const2),
        ],
        out_specs=pl.BlockSpec((1, MLSTM_CHUNK, MLSTM_WIDTH), lambda b, c: (b, c, 0)),
        out_shape=jax.ShapeDtypeStruct((bsz, t, MLSTM_WIDTH), jnp.float32),
        scratch_shapes=[pltpu.VMEM((HALO + MLSTM_CHUNK, MLSTM_WIDTH), jnp.float32),
                        pltpu.VMEM(hd, jnp.float32),
                        pltpu.VMEM((8, MLSTM_HEAD_DIM), jnp.float32),
                        pltpu.VMEM((8, 128), jnp.float32)],
        compiler_params=_params("parallel", "arbitrary"),
        name="mlstm",
    )(u_ml, u_sm, conv_w, conv_b, wq, wk, wv, gb_pad, norm_g, tril)


def _unit_lower_inverse(a_strict, idx_r, idx_c):
    n = a_strict.shape[0]
    eye = (idx_r == idx_c).astype(jnp.float32)
    t = eye + jnp.where((idx_r >> 1) == (idx_c >> 1), a_strict, 0.0)
    log_b = 1
    while (1 << log_b) < n:
        same_pair = (idx_r >> (log_b + 1)) == (idx_c >> (log_b + 1))
        e = jnp.where(same_pair, jnp.where((idx_r >> log_b) != (idx_c >> log_b), a_strict, 0.0), 0.0)
        t = t + _dot_hi(_dot_hi(t, e), t)
        log_b += 1
    return t


def _rwkv_kernel(u_ref, mu_ref, w0_ref, wup_ref, a0_ref, aup_ref, gup_ref, kk_ref, ka_ref, rk_ref,
                 lnw_ref, lnb_ref, tril_ref, hsum_ref, y_ref, ext_ref, s_ref, yh_ref):
    C, H, N, W = RWKV_CHUNK, RWKV_HEADS, RWKV_HEAD_DIM, RWKV_WIDTH
    first = pl.program_id(1) == 0

    @pl.when(first)
    def _():
        ext_ref[0:HALO, :] = jnp.zeros((HALO, U_RW_W), jnp.float32)
        s_ref[...] = jnp.zeros_like(s_ref)

    u = u_ref[0]
    ext_ref[HALO:HALO + C, :] = u
    u_prev = ext_ref[pl.ds(HALO - 1, C), :]
    ext_ref[0:HALO, :] = u[C - HALO:C, :]
    us = u + (u_prev - u) * mu_ref[...]
    r = us[:, 0:W]
    k = us[:, W:2 * W]
    v = us[:, 2 * W:3 * W]
    wa_l = us[:, 3 * W:3 * W + 128]
    g_l = us[:, 3 * W + 128:3 * W + 256]
    w_log = -_softplus(-(w0_ref[...] + _dot(_bf(jnp.tanh(wa_l)), wup_ref[...]))) - 0.5
    ld = -jnp.exp(w_log)
    a = jax.nn.sigmoid(a0_ref[...] + _dot(_bf(wa_l), aup_ref[...]))
    g = _dot(_bf(jax.nn.sigmoid(g_l)), gup_ref[...])
    kk = k * kk_ref[...]
    kk = kk / jnp.maximum(jnp.sqrt(_dot_hi(kk * kk, hsum_ref[...])), 1e-6)
    k = k * (1.0 + (a - 1.0) * ka_ref[...])

    cl = _dot_hi(tril_ref[...], ld)
    cl_end = cl[C - 1:C, :]
    p_inv = jnp.exp(-cl)
    p_end = jnp.exp(cl_end - cl)
    a_t = -kk * jnp.exp(cl - ld)
    kka = kk * a
    b_t = kka * p_inv
    k_t = k * p_inv
    r_t = r * jnp.exp(cl)
    b_e = kka * p_end
    k_e = k * p_end
    p_c = jnp.exp(cl_end)

    idx_r = lax.broadcasted_iota(jnp.int32, (C, C), 0)
    idx_c = lax.broadcasted_iota(jnp.int32, (C, C), 1)
    strict = idx_r > idx_c
    incl = idx_r >= idx_c

    for h in range(H):
        sl = slice(h * N, (h + 1) * N)
        s0 = s_ref[h]
        at, bt, kt, rt, vh = a_t[:, sl], b_t[:, sl], k_t[:, sl], r_t[:, sl], v[:, sl]
        a_ab = jnp.where(strict, _dot_nt(at, bt), 0.0)
        a_ak = jnp.where(strict, _dot_nt(at, kt), 0.0)
        a_rb = jnp.where(incl, _dot_nt(rt, bt), 0.0)
        a_rk = jnp.where(incl, _dot_nt(rt, kt), 0.0)
        t_inv = _unit_lower_inverse(a_ab, idx_r, idx_c)
        uu = _dot_hi(t_inv, _dot_nt(at, s0) + _dot(a_ak, vh))
        yh_ref[:, sl] = _dot_nt(rt, s0) + _dot(a_rb, uu) + _dot(a_rk, vh)
        s_ref[h] = s0 * p_c[:, sl] + _dot_tn(uu, b_e[:, sl]) + _dot_tn(vh, k_e[:, sl])

    y = yh_ref[...]
    mean_mat = hsum_ref[...] * (1.0 / N)
    mu = _dot_hi(y, mean_mat)
    yc = y - mu
    var = _dot_hi(yc * yc, mean_mat)
    y = yc * lax.rsqrt(var + RWKV_GN_EPS) * lnw_ref[...] + lnb_ref[...]
    y = y + _dot_hi(r * k * rk_ref[...], hsum_ref[...]) * v
    y_ref[0] = y * g


def _rwkv(u_rw, mu, w0, wup_pad, a0, aup_pad, g_up, k_k, k_a, r_k, ln_w, ln_b, tril, hsum):
    bsz, t, _ = u_rw.shape
    nc = t // RWKV_CHUNK
    const = lambda b, c: (0, 0)
    vec = pl.BlockSpec((1, RWKV_WIDTH), const)
    lowrank = pl.BlockSpec((128, RWKV_WIDTH), const)
    return pl.pallas_call(
        _rwkv_kernel,
        grid=(bsz, nc),
        in_specs=[
            pl.BlockSpec((1, RWKV_CHUNK, U_RW_W), lambda b, c: (b, c, 0)),
            pl.BlockSpec((1, U_RW_W), const),
            vec, lowrank, vec, lowrank, lowrank, vec, vec, vec, vec, vec,
            pl.BlockSpec((RWKV_CHUNK, RWKV_CHUNK), const),
            pl.BlockSpec((RWKV_WIDTH, RWKV_WIDTH), const),
        ],
        out_specs=pl.BlockSpec((1, RWKV_CHUNK, RWKV_WIDTH), lambda b, c: (b, c, 0)),
        out_shape=jax.ShapeDtypeStruct((bsz, t, RWKV_WIDTH), jnp.float32),
        scratch_shapes=[pltpu.VMEM((HALO + RWKV_CHUNK, U_RW_W), jnp.float32),
                        pltpu.VMEM((RWKV_HEADS, RWKV_HEAD_DIM, RWKV_HEAD_DIM), jnp.float32),
                        pltpu.VMEM((RWKV_CHUNK, RWKV_WIDTH), jnp.float32)],
        compiler_params=_params("parallel", "arbitrary"),
        name="rwkv",
    )(u_rw, mu, w0, wup_pad, a0, aup_pad, g_up, k_k, k_a, r_k, ln_w, ln_b, tril, hsum)


OUT_TM = 512


def _outproj_kernel(x_ref, ys_ref, ym_ref, yr_ref, ws_ref, wm_ref, wr_ref, o_ref):
    o_ref[...] = (x_ref[...] + _dot(_bf(ys_ref[...]), ws_ref[...]) + _dot(_bf(ym_ref[...]), wm_ref[...])
                  + _dot(_bf(yr_ref[...]), wr_ref[...]))


def _outproj(x2, y_ssd, y_ml, y_rw, w_s, w_m, w_r):
    n = x2.shape[0]
    tok = lambda w: pl.BlockSpec((OUT_TM, w), lambda i: (i, 0))
    wgt = lambda w: pl.BlockSpec((w, D_MODEL), lambda i: (0, 0))
    return pl.pallas_call(
        _outproj_kernel,
        grid=(n // OUT_TM,),
        in_specs=[tok(D_MODEL), tok(SSD_WIDTH), tok(MLSTM_WIDTH), tok(RWKV_WIDTH),
                  wgt(SSD_WIDTH), wgt(MLSTM_WIDTH), wgt(RWKV_WIDTH)],
        out_specs=tok(D_MODEL),
        out_shape=jax.ShapeDtypeStruct((n, D_MODEL), jnp.float32),
        compiler_params=_params("parallel"),
        name="outproj",
    )(x2, y_ssd, y_ml, y_rw, w_s, w_m, w_r)


def _pad_lanes(vec, offset, width=U_SM_W):
    out = jnp.zeros((1, width), jnp.float32)
    return out.at[0, offset:offset + vec.shape[0]].set(vec)


def kernel(x, ffn1_norm, ffn1_w_gate_up, ffn1_w_down, mix_norm, w_in, ssd_conv_w, ssd_conv_b, ssd_dt_bias, ssd_a_log, ssd_d, ssd_norm, mlstm_conv_w, mlstm_conv_b, mlstm_wq, mlstm_wk, mlstm_wv, mlstm_gate_bias, mlstm_norm, rwkv_shift_mu, rwkv_w0, rwkv_w_up, rwkv_a0, rwkv_a_up, rwkv_g_up, rwkv_k_k, rwkv_k_a, rwkv_r_k, rwkv_ln_w, rwkv_ln_b, w_out, ffn2_norm, ffn2_w_gate_up, ffn2_w_down, final_norm):
    bsz, t, d = x.shape
    depth = w_in.shape[0]
    n = bsz * t
    f32 = jnp.float32
    row = lambda v: v.reshape(1, -1).astype(f32)

    tril_l = jnp.tril(jnp.ones((SSD_CHUNK, SSD_CHUNK), f32))
    tril_c = jnp.tril(jnp.ones((RWKV_CHUNK, RWKV_CHUNK), f32))
    head_id = jnp.arange(RWKV_WIDTH) // RWKV_HEAD_DIM
    hsum = (head_id[:, None] == head_id[None, :]).astype(f32)
    final_g = row(final_norm)

    x2 = x.reshape(n, d)
    for l in range(depth):
        wl = w_in[l]
        o_ml = D_SSD_IN
        o_rw = D_SSD_IN + D_MLSTM_IN
        w_ssd = _bf(wl[:, 0:U_SSD_W])
        w_ml = _bf(wl[:, o_ml:o_ml + U_ML_W])
        w_rw = _bf(wl[:, o_rw:o_rw + U_RW_W])
        w_sm = _bf(jnp.concatenate(
            [wl[:, U_SSD_W:D_SSD_IN], wl[:, o_ml + U_ML_W:o_rw],
             jnp.zeros((d, U_SM_W - SSD_HEADS - 2 * MLSTM_HEADS), f32)], axis=1))

        x2 = _ffn(x2, row(ffn1_norm[l]), _bf(ffn1_w_gate_up[l]), _bf(ffn1_w_down[l]), final_g, False)

        u_ssd, u_ml, u_rw, u_sm = _inproj(x2, row(mix_norm[l]), w_ssd, w_ml, w_rw, w_sm)
        u_sm3 = u_sm.reshape(bsz, t, U_SM_W)

        y_ssd = _ssd(
            u_ssd.reshape(bsz, t, U_SSD_W), u_sm3, ssd_conv_w[l], row(ssd_conv_b[l]),
            _pad_lanes(ssd_dt_bias[l], SM_DT), _pad_lanes(-jnp.exp(ssd_a_log[l]), SM_DT),
            row(jnp.repeat(ssd_d[l], SSD_HEAD_DIM)), row(ssd_norm[l]), tril_l)

        y_ml = _mlstm(
            u_ml.reshape(bsz, t, U_ML_W), u_sm3, mlstm_conv_w[l], row(mlstm_conv_b[l]),
            _bf(mlstm_wq[l]), _bf(mlstm_wk[l]), _bf(mlstm_wv[l]),
            _pad_lanes(mlstm_gate_bias[l], SM_I), row(mlstm_norm[l]), tril_l)

        zeros64 = jnp.zeros((64, RWKV_WIDTH), f32)
        y_rw = _rwkv(
            u_rw.reshape(bsz, t, U_RW_W), row(rwkv_shift_mu[l]), row(rwkv_w0[l]),
            _bf(jnp.concatenate([rwkv_w_up[l], zeros64], axis=0)), row(rwkv_a0[l]),
            _bf(jnp.concatenate([zeros64, rwkv_a_up[l]], axis=0)), _bf(rwkv_g_up[l]),
            row(rwkv_k_k[l]), row(rwkv_k_a[l]), row(rwkv_r_k[l]), row(rwkv_ln_w[l]), row(rwkv_ln_b[l]),
            tril_c, hsum)

        wo = w_out[l]
        x2 = _outproj(x2, y_ssd.reshape(n, SSD_WIDTH), y_ml.reshape(n, MLSTM_WIDTH), y_rw.reshape(n, RWKV_WIDTH),
                      _bf(wo[0:SSD_WIDTH]), _bf(wo[SSD_WIDTH:SSD_WIDTH + MLSTM_WIDTH]),
                      _bf(wo[SSD_WIDTH + MLSTM_WIDTH:]))

        x2 = _ffn(x2, row(ffn2_norm[l]), _bf(ffn2_w_gate_up[l]), _bf(ffn2_w_down[l]), final_g, l == depth - 1)
    return x2.reshape(bsz, t, d)
```

```python
import functools

import jax
import jax.numpy as jnp
from jax import lax
from jax.experimental import pallas as pl
from jax.experimental.pallas import tpu as pltpu

D_MODEL = 1024
D_FF = 2816
SSD_WIDTH = 1024
SSD_HEAD_DIM = 64
SSD_HEADS = 16
SSD_GROUPS = 2
SSD_STATE = 64
SSD_CHUNK = 128
D_XBC = SSD_WIDTH + 2 * SSD_GROUPS * SSD_STATE
MLSTM_WIDTH = 512
MLSTM_HEADS = 4
MLSTM_HEAD_DIM = 128
MLSTM_CHUNK = 128
RWKV_WIDTH = 512
RWKV_HEADS = 8
RWKV_HEAD_DIM = 64
RWKV_CHUNK = 64
CONV_K = 4
EPS = 1e-6
RWKV_GN_EPS = 64e-5

D_SSD_IN = SSD_WIDTH + D_XBC + SSD_HEADS
D_MLSTM_IN = 2 * MLSTM_WIDTH + 2 * MLSTM_HEADS
D_RWKV_IN = 3 * RWKV_WIDTH + 64 + 64 + 128

U_SSD_W = SSD_WIDTH + D_XBC
U_ML_W = 2 * MLSTM_WIDTH
U_RW_W = D_RWKV_IN
U_SM_W = 128
SM_DT = 0
SM_I = SSD_HEADS
SM_F = SSD_HEADS + MLSTM_HEADS

HALO = 8
V7X_VMEM_LIMIT = 56 * 1024 * 1024
HIGHEST = lax.Precision.HIGHEST


def _silu(x):
    return x * jax.nn.sigmoid(x)


def _softplus(x):
    return jnp.maximum(x, 0.0) + jnp.log1p(jnp.exp(-jnp.abs(x)))


def _log_sigmoid(x):
    return -_softplus(-x)


def _rms(x, g):
    return x * lax.rsqrt(jnp.mean(x * x, axis=-1, keepdims=True) + EPS) * g


def _dot(a, b):
    return jnp.dot(a, b, preferred_element_type=jnp.float32)


def _dot_hi(a, b):
    return jnp.dot(a, b, preferred_element_type=jnp.float32, precision=HIGHEST)


def _dot_nt(a, b):
    return lax.dot_general(a, b, (((1,), (1,)), ((), ())), preferred_element_type=jnp.float32)


def _dot_tn(a, b):
    return lax.dot_general(a, b, (((0,), (0,)), ((), ())), preferred_element_type=jnp.float32)


def _bf(x):
    return x.astype(jnp.bfloat16)


def _params(*sem):
    return pltpu.CompilerParams(dimension_semantics=sem, vmem_limit_bytes=V7X_VMEM_LIMIT)


FFN_TM = 1024
FFN_TF = 256


def _ffn_kernel(x_ref, g_ref, wg_ref, wu_ref, wd_ref, fg_ref, o_ref, h_ref, acc_ref, *, final_norm):
    f = pl.program_id(1)

    @pl.when(f == 0)
    def _():
        h_ref[...] = _bf(_rms(x_ref[...], g_ref[...]))
        acc_ref[...] = jnp.zeros_like(acc_ref)

    h = h_ref[...]
    gate = _dot(h, wg_ref[...])
    up = _dot(h, wu_ref[...])
    acc_ref[...] += _dot(_bf(_silu(gate) * up), wd_ref[...])

    @pl.when(f == pl.num_programs(1) - 1)
    def _():
        y = x_ref[...] + 0.5 * acc_ref[...]
        if final_norm:
            y = _rms(y, fg_ref[...])
        o_ref[...] = y


def _ffn(x2, g, w_gate_up, w_down, final_g, final_norm):
    n = x2.shape[0]
    nf = D_FF // FFN_TF
    return pl.pallas_call(
        functools.partial(_ffn_kernel, final_norm=final_norm),
        grid=(n // FFN_TM, nf),
        in_specs=[
            pl.BlockSpec((FFN_TM, D_MODEL), lambda i, f: (i, 0)),
            pl.BlockSpec((1, D_MODEL), lambda i, f: (0, 0)),
            pl.BlockSpec((D_MODEL, FFN_TF), lambda i, f: (0, f)),
            pl.BlockSpec((D_MODEL, FFN_TF), lambda i, f: (0, f + nf)),
            pl.BlockSpec((FFN_TF, D_MODEL), lambda i, f: (f, 0)),
            pl.BlockSpec((1, D_MODEL), lambda i, f: (0, 0)),
        ],
        out_specs=pl.BlockSpec((FFN_TM, D_MODEL), lambda i, f: (i, 0)),
        out_shape=jax.ShapeDtypeStruct((n, D_MODEL), jnp.float32),
        scratch_shapes=[pltpu.VMEM((FFN_TM, D_MODEL), jnp.bfloat16),
                        pltpu.VMEM((FFN_TM, D_MODEL), jnp.float32)],
        compiler_params=_params("parallel", "arbitrary"),
        name="ffn",
    )(x2, g, w_gate_up, w_gate_up, w_down, final_g)


INP_TM = 256


def _inproj_kernel(x_ref, g_ref, wssd_ref, wml_ref, wrw_ref, wsm_ref, ossd_ref, oml_ref, orw_ref, osm_ref):
    h = _bf(_rms(x_ref[...], g_ref[...]))
    ossd_ref[...] = _dot(h, wssd_ref[...])
    oml_ref[...] = _dot(h, wml_ref[...])
    orw_ref[...] = _dot(h, wrw_ref[...])
    osm_ref[...] = _dot(h, wsm_ref[...])


def _inproj(x2, g, w_ssd, w_ml, w_rw, w_sm):
    n = x2.shape[0]
    widths = (U_SSD_W, U_ML_W, U_RW_W, U_SM_W)
    return pl.pallas_call(
        _inproj_kernel,
        grid=(n // INP_TM,),
        in_specs=[pl.BlockSpec((INP_TM, D_MODEL), lambda i: (i, 0)),
                  pl.BlockSpec((1, D_MODEL), lambda i: (0, 0))]
                 + [pl.BlockSpec((D_MODEL, w), lambda i: (0, 0)) for w in widths],
        out_specs=[pl.BlockSpec((INP_TM, w), lambda i: (i, 0)) for w in widths],
        out_shape=[jax.ShapeDtypeStruct((n, w), jnp.float32) for w in widths],
        compiler_params=_params("parallel"),
        name="inproj",
    )(x2, g, w_ssd, w_ml, w_rw, w_sm)


def _causal_conv(ext_ref, cur, w_ref, b_ref, first):
    rows = cur.shape[0]

    @pl.when(first)
    def _():
        ext_ref[0:HALO, :] = jnp.zeros((HALO, cur.shape[1]), jnp.float32)

    ext_ref[HALO:HALO + rows, :] = cur
    acc = b_ref[...] + cur * w_ref[CONV_K - 1:CONV_K, :]
    for j in range(CONV_K - 1):
        acc = acc + ext_ref[pl.ds(HALO - (CONV_K - 1) + j, rows), :] * w_ref[j:j + 1, :]
    ext_ref[0:HALO, :] = cur[rows - HALO:rows, :]
    return acc


def _ssd_kernel(u_ref, sm_ref, cw_ref, cb_ref, dtb_ref, nega_ref, dskip_ref, ng_ref, tril_ref,
                y_ref, ext_ref, st_ref):
    L, P, N, G = SSD_CHUNK, SSD_HEAD_DIM, SSD_STATE, SSD_GROUPS
    J = SSD_HEADS // G
    first = pl.program_id(1) == 0

    @pl.when(first)
    def _():
        st_ref[...] = jnp.zeros_like(st_ref)

    z = u_ref[0, :, 0:SSD_WIDTH]
    xbc = _silu(_causal_conv(ext_ref, u_ref[0, :, SSD_WIDTH:U_SSD_W], cw_ref, cb_ref, first))
    xs = xbc[:, 0:SSD_WIDTH]
    dt = _softplus(sm_ref[0] + dtb_ref[...])
    a = dt * nega_ref[...]
    a_cum = _dot_hi(tril_ref[...], a)
    a_cum_t = a_cum.T
    exp_a = jnp.exp(a_cum)
    dte = jnp.exp(a_cum[L - 1:L, :] - a_cum)
    row = lax.broadcasted_iota(jnp.int32, (L, L), 0)
    col = lax.broadcasted_iota(jnp.int32, (L, L), 1)
    causal = row >= col

    for g in range(G):
        b_g = xbc[:, SSD_WIDTH + g * N:SSD_WIDTH + (g + 1) * N]
        c_g = xbc[:, SSD_WIDTH + G * N + g * N:SSD_WIDTH + G * N + (g + 1) * N]
        cb = _dot_nt(_bf(c_g), _bf(b_g))
        s_prev = st_ref[g]
        y_off = _dot(_bf(c_g), _bf(s_prev))
        xdd_parts = []
        for j in range(J):
            h = g * J + j
            x_h = xs[:, h * P:(h + 1) * P]
            x_dt = x_h * dt[:, h:h + 1]
            seg = a_cum[:, h:h + 1] - a_cum_t[h:h + 1, :]
            m = cb * jnp.exp(jnp.where(causal, seg, -jnp.inf))
            y_h = (_dot(_bf(m), _bf(x_dt)) + y_off[:, j * P:(j + 1) * P] * exp_a[:, h:h + 1]
                   + x_h * dskip_ref[:, h * P:(h + 1) * P])
            y_ref[0, :, h * P:(h + 1) * P] = y_h
            xdd_parts.append(x_dt * dte[:, h:h + 1])
        xdd = jnp.concatenate(xdd_parts, axis=1)
        new_st = _dot_tn(_bf(b_g), _bf(xdd))
        cd = jnp.concatenate(
            [jnp.broadcast_to(exp_a[L - 1:L, g * J + j:g * J + j + 1], (1, P)) for j in range(J)], axis=1)
        st_ref[g] = s_prev * cd + new_st

    y = y_ref[0] * _silu(z)
    gw = SSD_WIDTH // G
    for g in range(G):
        yg = y[:, g * gw:(g + 1) * gw]
        y_ref[0, :, g * gw:(g + 1) * gw] = _rms(yg, ng_ref[:, g * gw:(g + 1) * gw])


def _ssd(u_ssd, u_sm, conv_w, conv_b, dtb_pad, nega_pad, dskip_wide, norm_g, tril):
    bsz, t, _ = u_ssd.shape
    nc = t // SSD_CHUNK
    const = lambda b, c: (0, 0)
    return pl.pallas_call(
        _ssd_kernel,
        grid=(bsz, nc),
        in_specs=[
            pl.BlockSpec((1, SSD_CHUNK, U_SSD_W), lambda b, c: (b, c, 0)),
            pl.BlockSpec((1, SSD_CHUNK, U_SM_W), lambda b, c: (b, c, 0)),
            pl.BlockSpec((CONV_K, D_XBC), const),
            pl.BlockSpec((1, D_XBC), const),
            pl.BlockSpec((1, U_SM_W), const),
            pl.BlockSpec((1, U_SM_W), const),
            pl.BlockSpec((1, SSD_WIDTH), const),
            pl.BlockSpec((1, SSD_WIDTH), const),
            pl.BlockSpec((SSD_CHUNK, SSD_CHUNK), const),
        ],
        out_specs=pl.BlockSpec((1, SSD_CHUNK, SSD_WIDTH), lambda b, c: (b, c, 0)),
        out_shape=jax.ShapeDtypeStruct((bsz, t, SSD_WIDTH), jnp.float32),
        scratch_shapes=[pltpu.VMEM((HALO + SSD_CHUNK, D_XBC), jnp.float32),
                        pltpu.VMEM((SSD_GROUPS, SSD_STATE, SSD_WIDTH // SSD_GROUPS), jnp.float32)],
        compiler_params=_params("parallel", "arbitrary"),
        name="ssd",
    )(u_ssd, u_sm, conv_w, conv_b, dtb_pad, nega_pad, dskip_wide, norm_g, tril)


def _mlstm_kernel(u_ref, sm_ref, cw_ref, cb_ref, wq_ref, wk_ref, wv_ref, gb_ref, ng_ref, tril_ref,
                  y_ref, ext_ref, c_ref, n_ref, m_ref):
    L, H, Dh = MLSTM_CHUNK, MLSTM_HEADS, MLSTM_HEAD_DIM
    first = pl.program_id(1) == 0

    @pl.when(first)
    def _():
        c_ref[...] = jnp.zeros_like(c_ref)
        n_ref[...] = jnp.zeros_like(n_ref)
        m_ref[...] = jnp.zeros_like(m_ref)

    xm = u_ref[0, :, 0:MLSTM_WIDTH]
    o_pre = u_ref[0, :, MLSTM_WIDTH:U_ML_W]
    xc = _silu(_causal_conv(ext_ref, xm, cw_ref, cb_ref, first))
    gates = sm_ref[0] + gb_ref[...]
    log_f = _log_sigmoid(gates)
    bcum = _dot_hi(tril_ref[...], log_f)
    bcum_t = bcum.T
    gates_t = gates.T
    row = lax.broadcasted_iota(jnp.int32, (L, L), 0)
    col = lax.broadcasted_iota(jnp.int32, (L, L), 1)
    causal = row >= col

    for h in range(H):
        sl = slice(h * Dh, (h + 1) * Dh)
        xc_h = _bf(xc[:, sl])
        q = _dot(xc_h, wq_ref[h])
        k = _dot(xc_h, wk_ref[h]) * (Dh ** -0.5)
        v = _dot(_bf(xm[:, sl]), wv_ref[h])
        b_col = bcum[:, SM_F + h:SM_F + h + 1]
        b_row = bcum_t[SM_F + h:SM_F + h + 1, :]
        i_col = gates[:, SM_I + h:SM_I + h + 1]
        i_row = gates_t[SM_I + h:SM_I + h + 1, :]
        b_end = bcum[L - 1:L, SM_F + h:SM_F + h + 1]
        m_st = m_ref[h:h + 1, 0:1]
        n_st = n_ref[h:h + 1, :]
        c_st = c_ref[h]

        d_log = jnp.where(causal, b_col - b_row + i_row, -jnp.inf)
        m_intra = jnp.max(d_log, axis=1, keepdims=True)
        inter_log = b_col + m_st
        m_t = jnp.maximum(inter_log, m_intra)
        w_intra = jnp.exp(d_log - m_t)
        w_inter = jnp.exp(inter_log - m_t)
        qb, kb, vb = _bf(q), _bf(k), _bf(v)
        qk = _dot_nt(qb, kb) * w_intra
        num = _dot(_bf(qk), vb) + _dot(qb, _bf(c_st)) * w_inter
        den = (jnp.sum(qk, axis=1, keepdims=True)
               + jnp.sum(q * n_st, axis=1, keepdims=True) * w_inter)
        hh = num / jnp.maximum(jnp.abs(den), jnp.exp(-m_t))
        hh = hh * jax.nn.sigmoid(o_pre[:, sl])
        y_ref[0, :, sl] = _rms(hh, ng_ref[:, sl])

        end_log = b_end - b_col + i_col
        m_loc = jnp.max(end_log, axis=0, keepdims=True)
        kw = k * jnp.exp(end_log - m_loc)
        c_loc = _dot_tn(_bf(kw), vb)
        n_loc = jnp.sum(kw, axis=0, keepdims=True)
        m_new = jnp.maximum(b_end + m_st, m_loc)
        s_old = jnp.exp(b_end + m_st - m_new)
        s_new = jnp.exp(m_loc - m_new)
        c_ref[h] = c_st * s_old + c_loc * s_new
        n_ref[h:h + 1, :] = n_st * s_old + n_loc * s_new
        m_ref[h:h + 1, :] = jnp.broadcast_to(m_new, (1, 128))


def _mlstm(u_ml, u_sm, conv_w, conv_b, wq, wk, wv, gb_pad, norm_g, tril):
    bsz, t, _ = u_ml.shape
    nc = t // MLSTM_CHUNK
    const2 = lambda b, c: (0, 0)
    const3 = lambda b, c: (0, 0, 0)
    hd = (MLSTM_HEADS, MLSTM_HEAD_DIM, MLSTM_HEAD_DIM)
    return pl.pallas_call(
        _mlstm_kernel,
        grid=(bsz, nc),
        in_specs=[
            pl.BlockSpec((1, MLSTM_CHUNK, U_ML_W), lambda b, c: (b, c, 0)),
            pl.BlockSpec((1, MLSTM_CHUNK, U_SM_W), lambda b, c: (b, c, 0)),
            pl.BlockSpec((CONV_K, MLSTM_WIDTH), const2),
            pl.BlockSpec((1, MLSTM_WIDTH), const2),
            pl.BlockSpec(hd, const3),
            pl.BlockSpec(hd, const3),
            pl.BlockSpec(hd, const3),
            pl.BlockSpec((1, U_SM_W), const2),
            pl.BlockSpec((1, MLSTM_WIDTH), const2),
            pl.BlockSpec((MLSTM_CHUNK, MLSTM_CHUNK), const2),
        ],
        out_specs=pl.BlockSpec((1, MLSTM_CHUNK, MLSTM_WIDTH), lambda b, c: (b, c, 0)),
        out_shape=jax.ShapeDtypeStruct((bsz, t, MLSTM_WIDTH), jnp.float32),
        scratch_shapes=[pltpu.VMEM((HALO + MLSTM_CHUNK, MLSTM_WIDTH), jnp.float32),
                        pltpu.VMEM(hd, jnp.float32),
                        pltpu.VMEM((8, MLSTM_HEAD_DIM), jnp.float32),
                        pltpu.VMEM((8, 128), jnp.float32)],
        compiler_params=_params("parallel", "arbitrary"),
        name="mlstm",
    )(u_ml, u_sm, conv_w, conv_b, wq, wk, wv, gb_pad, norm_g, tril)


def _unit_lower_inverse(a_list, idx_r, idx_c, n):
    eye = (idx_r == idx_c).astype(jnp.float32)
    base = (idx_r >> 1) == (idx_c >> 1)
    ts = [eye + jnp.where(base, a, 0.0) for a in a_list]
    log_b = 1
    while (1 << log_b) < n:
        join = ((idx_r >> (log_b + 1)) == (idx_c >> (log_b + 1))) & ((idx_r >> log_b) != (idx_c >> log_b))
        tbs = [_bf(t) for t in ts]
        tes = [_bf(_dot(tb, _bf(jnp.where(join, a, 0.0)))) for tb, a in zip(tbs, a_list)]
        ts = [t + _dot(te, tb) for t, te, tb in zip(ts, tes, tbs)]
        log_b += 1
    return ts


def _split3(x):
    hi = _bf(x)
    r1 = x - hi.astype(jnp.float32)
    mid = _bf(r1)
    lo = _bf(r1 - mid.astype(jnp.float32))
    return hi, mid, lo


def _head_sum(x, hsum_bf):
    rows = x.shape[0]
    hi = _bf(x)
    lo = _bf(x - hi.astype(jnp.float32))
    s = _dot(jnp.concatenate([hi, lo], axis=0), hsum_bf)
    return s[0:rows] + s[rows:2 * rows]


def _rwkv_kernel(u_ref, mu_ref, w0_ref, wup_ref, a0_ref, aup_ref, gup_ref, kk_ref, ka_ref, rk_ref,
                 lnw_ref, lnb_ref, tril_ref, hsum_ref, y_ref, ext_ref, s_ref):
    C, H, N, W = RWKV_CHUNK, RWKV_HEADS, RWKV_HEAD_DIM, RWKV_WIDTH
    nb = u_ref.shape[0]

    @pl.when(pl.program_id(0) == 0)
    def _():
        ext_ref[:, 0:HALO, :] = jnp.zeros((nb, HALO, U_RW_W), jnp.float32)
        s_ref[...] = jnp.zeros_like(s_ref)

    u_rows, prev_rows = [], []
    for b in range(nb):
        ub = u_ref[b]
        ext_ref[b, HALO:HALO + C, :] = ub
        prev_rows.append(ext_ref[b, pl.ds(HALO - 1, C), :])
        ext_ref[b, 0:HALO, :] = ub[C - HALO:C, :]
        u_rows.append(ub)
    u = jnp.concatenate(u_rows, axis=0)
    u_prev = jnp.concatenate(prev_rows, axis=0)
    us = u + (u_prev - u) * mu_ref[...]
    r = us[:, 0:W]
    k = us[:, W:2 * W]
    v = us[:, 2 * W:3 * W]
    wa_l = us[:, 3 * W:3 * W + 128]
    g_l = us[:, 3 * W + 128:3 * W + 256]
    w_log = -_softplus(-(w0_ref[...] + _dot(_bf(jnp.tanh(wa_l)), wup_ref[...]))) - 0.5
    ld = -jnp.exp(w_log)
    a = jax.nn.sigmoid(a0_ref[...] + _dot(_bf(wa_l), aup_ref[...]))
    g = _dot(_bf(jax.nn.sigmoid(g_l)), gup_ref[...])
    hsum = hsum_ref[...]
    kk = k * kk_ref[...]
    kk = kk / jnp.maximum(jnp.sqrt(_head_sum(kk * kk, hsum)), 1e-6)
    k = k * (1.0 + (a - 1.0) * ka_ref[...])

    ld_parts = _split3(ld)
    cl3 = _dot(tril_ref[...], jnp.concatenate(ld_parts, axis=1))
    cl = cl3[:, 0:W] + cl3[:, W:2 * W] + cl3[:, 2 * W:3 * W]
    cl_end = jnp.concatenate([jnp.broadcast_to(cl[(b + 1) * C - 1:(b + 1) * C, :], (C, W)) for b in range(nb)],
                             axis=0)
    p_inv = jnp.exp(-cl)
    p_end = jnp.exp(cl_end - cl)
    a_t = -kk * jnp.exp(cl - ld)
    kka = kk * a
    b_t = kka * p_inv
    k_t = k * p_inv
    r_t = r * jnp.exp(cl)
    b_e = kka * p_end
    k_e = k * p_end

    PW = 2 * N
    idx_r = lax.broadcasted_iota(jnp.int32, (PW, PW), 0)
    idx_c = lax.broadcasted_iota(jnp.int32, (PW, PW), 1)
    same_head = (idx_r >> 6) == (idx_c >> 6)
    strict = same_head & (idx_r > idx_c)
    incl = same_head & (idx_r >= idx_c)
    lane = lax.broadcasted_iota(jnp.int32, (C, PW), 1)
    head0 = lane < N
    ones_c = jnp.ones((C, PW), jnp.bfloat16)

    def stack(x):
        return _bf(jnp.concatenate([jnp.where(head0, x, 0.0), jnp.where(head0, 0.0, x)], axis=0))

    probs = [(b, p) for b in range(nb) for p in range(H // 2)]
    pairs = range(len(probs))
    cut = lambda x, b, p: x[b * C:(b + 1) * C, p * PW:(p + 1) * PW]
    sts =[s_ref[q] for q in pairs]
    at_s = [stack(cut(a_t, b, p)) for b, p in probs]
    rt_s = [stack(cut(r_t, b, p)) for b, p in probs]
    v_s = [stack(cut(v, b, p)) for b, p in probs]
    aas = []
    for q, (b, p) in zip(pairs, probs):
        bt, kt = _bf(cut(b_t, b, p)), _bf(cut(k_t, b, p))
        aas.append(_dot_nt(jnp.concatenate([at_s[q], rt_s[q]], axis=0), jnp.concatenate([bt, bt, kt, kt], axis=0)))
    t_inv = _unit_lower_inverse([jnp.where(strict, aa[0:PW, 0:PW], 0.0) for aa in aas], idx_r, idx_c, C)
    rhs = [_dot(jnp.concatenate([at_s[p], _bf(jnp.where(strict, aas[p][0:PW, PW:2 * PW], 0.0))], axis=1),
                jnp.concatenate([_bf(sts[p]), v_s[p]], axis=0)) for p in pairs]
    uu = [_bf(_dot(_bf(t_inv[p]), _bf(rhs[p]))) for p in pairs]
    y_parts = []
    for p in pairs:
        a_rb = jnp.where(incl, aas[p][PW:2 * PW, 0:PW], 0.0)
        a_rk = jnp.where(incl, aas[p][PW:2 * PW, PW:2 * PW], 0.0)
        yy = _dot(jnp.concatenate([rt_s[p], _bf(a_rb), _bf(a_rk)], axis=1),
                  jnp.concatenate([_bf(sts[p]), uu[p], v_s[p]], axis=0))
        y_parts.append(yy[0:C] + yy[C:2 * C])
    for q, (b, p) in zip(pairs, probs):
        ld_end_col = sum(_dot_tn(cut(part, b, p), ones_c) for part in ld_parts)
        s_ref[q] = sts[q] * jnp.exp(ld_end_col) + _dot_tn(
            jnp.concatenate([stack(cut(b_e, b, p)), stack(cut(k_e, b, p))], axis=0),
            jnp.concatenate([uu[q], v_s[q]], axis=0))

    npair = H // 2
    y = jnp.concatenate([jnp.concatenate(y_parts[b * npair:(b + 1) * npair], axis=1) for b in range(nb)], axis=0)
    mu = _head_sum(y, hsum) * (1.0 / N)
    yc = y - mu
    var = _head_sum(yc * yc, hsum) * (1.0 / N)
    y = yc * lax.rsqrt(var + RWKV_GN_EPS) * lnw_ref[...] + lnb_ref[...]
    y = y + _head_sum(r * k * rk_ref[...], hsum) * v
    y = y * g
    for b in range(nb):
        y_ref[b] = y[b * C:(b + 1) * C]


def _rwkv(u_rw, mu, w0, wup_pad, a0, aup_pad, g_up, k_k, k_a, r_k, ln_w, ln_b, tril, hsum):
    bsz, t, _ = u_rw.shape
    nc = t // RWKV_CHUNK
    const = lambda c: (0, 0)
    vec = pl.BlockSpec((1, RWKV_WIDTH), const)
    lowrank = pl.BlockSpec((128, RWKV_WIDTH), const)
    return pl.pallas_call(
        _rwkv_kernel,
        grid=(nc,),
        in_specs=[
            pl.BlockSpec((bsz, RWKV_CHUNK, U_RW_W), lambda c: (0, c, 0)),
            pl.BlockSpec((1, U_RW_W), const),
            vec, lowrank, vec, lowrank, lowrank, vec, vec, vec, vec, vec,
            pl.BlockSpec((bsz * RWKV_CHUNK, bsz * RWKV_CHUNK), const),
            pl.BlockSpec((RWKV_WIDTH, RWKV_WIDTH), const),
        ],
        out_specs=pl.BlockSpec((bsz, RWKV_CHUNK, RWKV_WIDTH), lambda c: (0, c, 0)),
        out_shape=jax.ShapeDtypeStruct((bsz, t, RWKV_WIDTH), jnp.float32),
        scratch_shapes=[pltpu.VMEM((bsz, HALO + RWKV_CHUNK, U_RW_W), jnp.float32),
                        pltpu.VMEM((bsz * RWKV_HEADS // 2, 2 * RWKV_HEAD_DIM, 2 * RWKV_HEAD_DIM), jnp.float32)],
        compiler_params=_params("arbitrary"),
        name="rwkv",
    )(u_rw, mu, w0, wup_pad, a0, aup_pad, g_up, k_k, k_a, r_k, ln_w, ln_b, tril, hsum)


OUT_TM = 512


def _outproj_kernel(x_ref, ys_ref, ym_ref, yr_ref, ws_ref, wm_ref, wr_ref, o_ref):
    o_ref[...] = (x_ref[...] + _dot(_bf(ys_ref[...]), ws_ref[...]) + _dot(_bf(ym_ref[...]), wm_ref[...])
                  + _dot(_bf(yr_ref[...]), wr_ref[...]))


def _outproj(x2, y_ssd, y_ml, y_rw, w_s, w_m, w_r):
    n = x2.shape[0]
    tok = lambda w: pl.BlockSpec((OUT_TM, w), lambda i: (i, 0))
    wgt = lambda w: pl.BlockSpec((w, D_MODEL), lambda i: (0, 0))
    return pl.pallas_call(
        _outproj_kernel,
        grid=(n // OUT_TM,),
        in_specs=[tok(D_MODEL), tok(SSD_WIDTH), tok(MLSTM_WIDTH), tok(RWKV_WIDTH),
                  wgt(SSD_WIDTH), wgt(MLSTM_WIDTH), wgt(RWKV_WIDTH)],
        out_specs=tok(D_MODEL),
        out_shape=jax.ShapeDtypeStruct((n, D_MODEL), jnp.float32),
        compiler_params=_params("parallel"),
        name="outproj",
    )(x2, y_ssd, y_ml, y_rw, w_s, w_m, w_r)


def _pad_lanes(vec, offset, width=U_SM_W):
    out = jnp.zeros((1, width), jnp.float32)
    return out.at[0, offset:offset + vec.shape[0]].set(vec)


def kernel(x, ffn1_norm, ffn1_w_gate_up, ffn1_w_down, mix_norm, w_in, ssd_conv_w, ssd_conv_b, ssd_dt_bias, ssd_a_log, ssd_d, ssd_norm, mlstm_conv_w, mlstm_conv_b, mlstm_wq, mlstm_wk, mlstm_wv, mlstm_gate_bias, mlstm_norm, rwkv_shift_mu, rwkv_w0, rwkv_w_up, rwkv_a0, rwkv_a_up, rwkv_g_up, rwkv_k_k, rwkv_k_a, rwkv_r_k, rwkv_ln_w, rwkv_ln_b, w_out, ffn2_norm, ffn2_w_gate_up, ffn2_w_down, final_norm):
    bsz, t, d = x.shape
    depth = w_in.shape[0]
    n = bsz * t
    f32 = jnp.float32
    row = lambda v: v.reshape(1, -1).astype(f32)

    tril_l = jnp.tril(jnp.ones((SSD_CHUNK, SSD_CHUNK), f32))
    tril_c = _bf(jnp.kron(jnp.eye(bsz, dtype=f32), jnp.tril(jnp.ones((RWKV_CHUNK, RWKV_CHUNK), f32))))
    head_id = jnp.arange(RWKV_WIDTH) // RWKV_HEAD_DIM
    hsum = _bf(head_id[:, None] == head_id[None, :])
    final_g = row(final_norm)

    x2 = x.reshape(n, d)
    for l in range(depth):
        wl = w_in[l]
        o_ml = D_SSD_IN
        o_rw = D_SSD_IN + D_MLSTM_IN
        w_ssd = _bf(wl[:, 0:U_SSD_W])
        w_ml = _bf(wl[:, o_ml:o_ml + U_ML_W])
        w_rw = _bf(wl[:, o_rw:o_rw + U_RW_W])
        w_sm = _bf(jnp.concatenate(
            [wl[:, U_SSD_W:D_SSD_IN], wl[:, o_ml + U_ML_W:o_rw],
             jnp.zeros((d, U_SM_W - SSD_HEADS - 2 * MLSTM_HEADS), f32)], axis=1))

        x2 = _ffn(x2, row(ffn1_norm[l]), _bf(ffn1_w_gate_up[l]), _bf(ffn1_w_down[l]), final_g, False)

        u_ssd, u_ml, u_rw, u_sm = _inproj(x2, row(mix_norm[l]), w_ssd, w_ml, w_rw, w_sm)
        u_sm3 = u_sm.reshape(bsz, t, U_SM_W)

        y_ssd = _ssd(
            u_ssd.reshape(bsz, t, U_SSD_W), u_sm3, ssd_conv_w[l], row(ssd_conv_b[l]),
            _pad_lanes(ssd_dt_bias[l], SM_DT), _pad_lanes(-jnp.exp(ssd_a_log[l]), SM_DT),
            row(jnp.repeat(ssd_d[l], SSD_HEAD_DIM)), row(ssd_norm[l]), tril_l)

        y_ml = _mlstm(
            u_ml.reshape(bsz, t, U_ML_W), u_sm3, mlstm_conv_w[l], row(mlstm_conv_b[l]),
            _bf(mlstm_wq[l]), _bf(mlstm_wk[l]), _bf(mlstm_wv[l]),
            _pad_lanes(mlstm_gate_bias[l], SM_I), row(mlstm_norm[l]), tril_l)

        zeros64 = jnp.zeros((64, RWKV_WIDTH), f32)
        y_rw = _rwkv(
            u_rw.reshape(bsz, t, U_RW_W), row(rwkv_shift_mu[l]), row(rwkv_w0[l]),
            _bf(jnp.concatenate([rwkv_w_up[l], zeros64], axis=0)), row(rwkv_a0[l]),
            _bf(jnp.concatenate([zeros64, rwkv_a_up[l]], axis=0)), _bf(rwkv_g_up[l]),
            row(rwkv_k_k[l]), row(rwkv_k_a[l]), row(rwkv_r_k[l]), row(rwkv_ln_w[l]), row(rwkv_ln_b[l]),
            tril_c, hsum)

        wo = w_out[l]
        x2 = _outproj(x2, y_ssd.reshape(n, SSD_WIDTH), y_ml.reshape(n, MLSTM_WIDTH), y_rw.reshape(n, RWKV_WIDTH),
                      _bf(wo[0:SSD_WIDTH]), _bf(wo[SSD_WIDTH:SSD_WIDTH + MLSTM_WIDTH]),
                      _bf(wo[SSD_WIDTH + MLSTM_WIDTH:]))

        x2 = _ffn(x2, row(ffn2_norm[l]), _bf(ffn2_w_gate_up[l]), _bf(ffn2_w_down[l]), final_g, l == depth - 1)
    return x2.reshape(bsz, t, d)
```

```python
import functools

import jax
import jax.numpy as jnp
from jax import lax
from jax.experimental import pallas as pl
from jax.experimental.pallas import tpu as pltpu

D_MODEL = 1024
D_FF = 2816
SSD_WIDTH = 1024
SSD_HEAD_DIM = 64
SSD_HEADS = 16
SSD_GROUPS = 2
SSD_STATE = 64
SSD_CHUNK = 128
D_XBC = SSD_WIDTH + 2 * SSD_GROUPS * SSD_STATE
MLSTM_WIDTH = 512
MLSTM_HEADS = 4
MLSTM_HEAD_DIM = 128
MLSTM_CHUNK = 128
RWKV_WIDTH = 512
RWKV_HEADS = 8
RWKV_HEAD_DIM = 64
RWKV_CHUNK = 64
CONV_K = 4
EPS = 1e-6
RWKV_GN_EPS = 64e-5

D_SSD_IN = SSD_WIDTH + D_XBC + SSD_HEADS
D_MLSTM_IN = 2 * MLSTM_WIDTH + 2 * MLSTM_HEADS
D_RWKV_IN = 3 * RWKV_WIDTH + 64 + 64 + 128

U_SSD_W = SSD_WIDTH + D_XBC
U_ML_W = 2 * MLSTM_WIDTH
U_RW_W = D_RWKV_IN
U_SM_W = 128
SM_DT = 0
SM_I = SSD_HEADS
SM_F = SSD_HEADS + MLSTM_HEADS

HALO = 8
V7X_VMEM_LIMIT = 56 * 1024 * 1024
HIGHEST = lax.Precision.HIGHEST


def _silu(x):
    return x * jax.nn.sigmoid(x)


def _softplus(x):
    return jnp.maximum(x, 0.0) + jnp.log1p(jnp.exp(-jnp.abs(x)))


def _log_sigmoid(x):
    return -_softplus(-x)


def _rms(x, g):
    return x * lax.rsqrt(jnp.mean(x * x, axis=-1, keepdims=True) + EPS) * g


def _dot(a, b):
    return jnp.dot(a, b, preferred_element_type=jnp.float32)


def _dot_hi(a, b):
    return jnp.dot(a, b, preferred_element_type=jnp.float32, precision=HIGHEST)


def _dot_nt(a, b):
    return lax.dot_general(a, b, (((1,), (1,)), ((), ())), preferred_element_type=jnp.float32)


def _dot_tn(a, b):
    return lax.dot_general(a, b, (((0,), (0,)), ((), ())), preferred_element_type=jnp.float32)


def _bf(x):
    return x.astype(jnp.bfloat16)


def _params(*sem):
    return pltpu.CompilerParams(dimension_semantics=sem, vmem_limit_bytes=V7X_VMEM_LIMIT)


FFN_TM = 1024
FFN_TF = 256


def _ffn_kernel(x_ref, g_ref, wg_ref, wu_ref, wd_ref, fg_ref, o_ref, h_ref, acc_ref, *, final_norm):
    f = pl.program_id(1)

    @pl.when(f == 0)
    def _():
        h_ref[...] = _bf(_rms(x_ref[...], g_ref[...]))
        acc_ref[...] = jnp.zeros_like(acc_ref)

    h = h_ref[...]
    gate = _dot(h, wg_ref[...])
    up = _dot(h, wu_ref[...])
    acc_ref[...] += _dot(_bf(_silu(gate) * up), wd_ref[...])

    @pl.when(f == pl.num_programs(1) - 1)
    def _():
        y = x_ref[...] + 0.5 * acc_ref[...]
        if final_norm:
            y = _rms(y, fg_ref[...])
        o_ref[...] = y


def _ffn(x2, g, w_gate_up, w_down, final_g, final_norm):
    n = x2.shape[0]
    nf = D_FF // FFN_TF
    return pl.pallas_call(
        functools.partial(_ffn_kernel, final_norm=final_norm),
        grid=(n // FFN_TM, nf),
        in_specs=[
            pl.BlockSpec((FFN_TM, D_MODEL), lambda i, f: (i, 0)),
            pl.BlockSpec((1, D_MODEL), lambda i, f: (0, 0)),
            pl.BlockSpec((D_MODEL, FFN_TF), lambda i, f: (0, f)),
            pl.BlockSpec((D_MODEL, FFN_TF), lambda i, f: (0, f + nf)),
            pl.BlockSpec((FFN_TF, D_MODEL), lambda i, f: (f, 0)),
            pl.BlockSpec((1, D_MODEL), lambda i, f: (0, 0)),
        ],
        out_specs=pl.BlockSpec((FFN_TM, D_MODEL), lambda i, f: (i, 0)),
        out_shape=jax.ShapeDtypeStruct((n, D_MODEL), jnp.float32),
        scratch_shapes=[pltpu.VMEM((FFN_TM, D_MODEL), jnp.bfloat16),
                        pltpu.VMEM((FFN_TM, D_MODEL), jnp.float32)],
        compiler_params=_params("parallel", "arbitrary"),
        name="ffn",
    )(x2, g, w_gate_up, w_gate_up, w_down, final_g)


INP_TM = 256


def _inproj_kernel(x_ref, g_ref, wssd_ref, wml_ref, wrw_ref, wsm_ref, ossd_ref, oml_ref, orw_ref, osm_ref):
    h = _bf(_rms(x_ref[...], g_ref[...]))
    ossd_ref[...] = _dot(h, wssd_ref[...])
    oml_ref[...] = _dot(h, wml_ref[...])
    orw_ref[...] = _dot(h, wrw_ref[...])
    osm_ref[...] = _dot(h, wsm_ref[...])


def _inproj(x2, g, w_ssd, w_ml, w_rw, w_sm):
    n = x2.shape[0]
    widths = (U_SSD_W, U_ML_W, U_RW_W, U_SM_W)
    return pl.pallas_call(
        _inproj_kernel,
        grid=(n // INP_TM,),
        in_specs=[pl.BlockSpec((INP_TM, D_MODEL), lambda i: (i, 0)),
                  pl.BlockSpec((1, D_MODEL), lambda i: (0, 0))]
                 + [pl.BlockSpec((D_MODEL, w), lambda i: (0, 0)) for w in widths],
        out_specs=[pl.BlockSpec((INP_TM, w), lambda i: (i, 0)) for w in widths],
        out_shape=[jax.ShapeDtypeStruct((n, w), jnp.float32) for w in widths],
        compiler_params=_params("parallel"),
        name="inproj",
    )(x2, g, w_ssd, w_ml, w_rw, w_sm)


def _causal_conv(ext_ref, cur, w_ref, b_ref, first):
    rows = cur.shape[0]

    @pl.when(first)
    def _():
        ext_ref[0:HALO, :] = jnp.zeros((HALO, cur.shape[1]), jnp.float32)

    ext_ref[HALO:HALO + rows, :] = cur
    acc = b_ref[...] + cur * w_ref[CONV_K - 1:CONV_K, :]
    for j in range(CONV_K - 1):
        acc = acc + ext_ref[pl.ds(HALO - (CONV_K - 1) + j, rows), :] * w_ref[j:j + 1, :]
    ext_ref[0:HALO, :] = cur[rows - HALO:rows, :]
    return acc


def _split_lanes(x, terms):
    parts, rest = [], x
    for i in range(terms):
        p = _bf(rest)
        parts.append(p)
        if i + 1 < terms:
            rest = rest - p.astype(jnp.float32)
    return jnp.concatenate(parts, axis=1)


def _split_rows(x, terms):
    parts, rest = [], x
    for i in range(terms):
        p = _bf(rest)
        parts.append(p)
        if i + 1 < terms:
            rest = rest - p.astype(jnp.float32)
    return jnp.concatenate(parts, axis=0)


def _ssd_kernel(u_ref, sm_ref, cw_ref, cb_ref, dtb_ref, nega_ref, dskip_ref, ng_ref, tril_ref, expand_ref,
                selcol_ref, y_ref, ext_ref, st_ref):
    L, P, N, G = SSD_CHUNK, SSD_HEAD_DIM, SSD_STATE, SSD_GROUPS
    H = SSD_HEADS
    first = pl.program_id(1) == 0

    @pl.when(first)
    def _():
        st_ref[...] = jnp.zeros_like(st_ref)

    z = u_ref[0, :, 0:SSD_WIDTH]
    xbc = _silu(_causal_conv(ext_ref, u_ref[0, :, SSD_WIDTH:U_SSD_W], cw_ref, cb_ref, first))
    xs = xbc[:, 0:SSD_WIDTH]
    b_in = xbc[:, SSD_WIDTH:SSD_WIDTH + G * N]
    c_in = xbc[:, SSD_WIDTH + G * N:SSD_WIDTH + 2 * G * N]
    dt = _softplus(sm_ref[0] + dtb_ref[...])
    a = dt * nega_ref[...]
    a_cum = _dot(tril_ref[...], _split_rows(a, 3))
    a_cum_t = a_cum.T
    a_cum2 = _split_lanes(a_cum, 2)
    a_wide = _dot(a_cum2, expand_ref[...])
    col_b = _dot(a_cum2, selcol_ref[...])
    dt_wide = _dot(_split_lanes(dt, 2), expand_ref[...])
    exp_a = jnp.exp(a_wide)
    x_dt = xs * dt_wide
    xdd = x_dt * jnp.exp(a_wide[L - 1:L, :] - a_wide)

    row = lax.broadcasted_iota(jnp.int32, (L, L), 0)
    col = lax.broadcasted_iota(jnp.int32, (L, L), 1)
    causal = row >= col
    group0 = col < N
    c_b = _bf(c_in)
    b_b = _bf(b_in)
    cbs = [_dot_nt(_bf(jnp.where(group0 if g == 0 else ~group0, c_in, 0.0)), b_b) for g in range(G)]
    st = st_ref[...]
    y_off = _dot(c_b, _bf(st))

    x_dt_b = _bf(x_dt)
    y_parts = []
    for hp in range(H // 2):
        cb_g = cbs[(2 * hp) // (H // G)]
        x_pair = x_dt_b[:, hp * 2 * P:(hp + 1) * 2 * P]
        halves = []
        for h in (2 * hp, 2 * hp + 1):
            seg = col_b[:, h * 128:(h + 1) * 128] - a_cum_t[h:h + 1, :]
            halves.append(_dot(_bf(cb_g * jnp.exp(jnp.where(causal, seg, -jnp.inf))), x_pair))
        y_parts.append(jnp.where(group0, halves[0], halves[1]))
    y = jnp.concatenate(y_parts, axis=1) + y_off * exp_a + xs * dskip_ref[...]

    srow = lax.broadcasted_iota(jnp.int32, (G * N, H * P), 0)
    scol = lax.broadcasted_iota(jnp.int32, (G * N, H * P), 1)
    own = (srow >= N) == (scol >= (H // G) * P)
    st_ref[...] = st * exp_a[L - 1:L, :] + jnp.where(own, _dot_tn(b_b, _bf(xdd)), 0.0)

    y = y * _silu(z)
    gw = SSD_WIDTH // G
    y_ref[0] = jnp.concatenate(
        [_rms(y[:, g * gw:(g + 1) * gw], ng_ref[:, g * gw:(g + 1) * gw]) for g in range(G)], axis=1)


def _ssd(u_ssd, u_sm, conv_w, conv_b, dtb_pad, nega_pad, dskip_wide, norm_g, tril3, expand, selcol):
    bsz, t, _ = u_ssd.shape
    nc = t // SSD_CHUNK
    const = lambda b, c: (0, 0)
    return pl.pallas_call(
        _ssd_kernel,
        grid=(bsz, nc),
        in_specs=[
            pl.BlockSpec((1, SSD_CHUNK, U_SSD_W), lambda b, c: (b, c, 0)),
            pl.BlockSpec((1, SSD_CHUNK, U_SM_W), lambda b, c: (b, c, 0)),
            pl.BlockSpec((CONV_K, D_XBC), const),
            pl.BlockSpec((1, D_XBC), const),
            pl.BlockSpec((1, U_SM_W), const),
            pl.BlockSpec((1, U_SM_W), const),
            pl.BlockSpec((1, SSD_WIDTH), const),
            pl.BlockSpec((1, SSD_WIDTH), const),
            pl.BlockSpec(tril3.shape, const),
            pl.BlockSpec(expand.shape, const),
            pl.BlockSpec(selcol.shape, const),
        ],
        out_specs=pl.BlockSpec((1, SSD_CHUNK, SSD_WIDTH), lambda b, c: (b, c, 0)),
        out_shape=jax.ShapeDtypeStruct((bsz, t, SSD_WIDTH), jnp.float32),
        scratch_shapes=[pltpu.VMEM((HALO + SSD_CHUNK, D_XBC), jnp.float32),
                        pltpu.VMEM((SSD_GROUPS * SSD_STATE, SSD_WIDTH), jnp.float32)],
        compiler_params=_params("parallel", "arbitrary"),
        name="ssd",
    )(u_ssd, u_sm, conv_w, conv_b, dtb_pad, nega_pad, dskip_wide, norm_g, tril3, expand, selcol)


def _mlstm_kernel(u_ref, sm_ref, cw_ref, cb_ref, wqk_ref, wv_ref, gb_ref, ng_ref, tril_ref, selcol_ref,
                  y_ref, ext_ref, cn_ref, m_ref):
    L, H, Dh = MLSTM_CHUNK, MLSTM_HEADS, MLSTM_HEAD_DIM
    first = pl.program_id(1) == 0

    @pl.when(first)
    def _():
        cn_ref[...] = jnp.zeros_like(cn_ref)
        m_ref[...] = jnp.zeros_like(m_ref)

    xm = u_ref[0, :, 0:MLSTM_WIDTH]
    o_pre = u_ref[0, :, MLSTM_WIDTH:U_ML_W]
    xc_b = _bf(_silu(_causal_conv(ext_ref, xm, cw_ref, cb_ref, first)))
    xm_b = _bf(xm)
    gates = sm_ref[0] + gb_ref[...]
    log_f = _log_sigmoid(gates)
    bcum = _dot(tril_ref[...], _split_rows(log_f, 3))
    bcum_t = bcum.T
    gates_t = gates.T
    col_f = _dot(_split_lanes(bcum, 2), selcol_ref[:, 0:H * 128])
    col_i = _dot(_split_lanes(gates, 2), selcol_ref[:, H * 128:2 * H * 128])
    row = lax.broadcasted_iota(jnp.int32, (L, L), 0)
    col = lax.broadcasted_iota(jnp.int32, (L, L), 1)
    causal = row >= col
    ones = jnp.ones((L, Dh), jnp.bfloat16)
    twice = lambda x: jnp.concatenate([x, x], axis=1)
    heads = range(H)
    sls = [slice(h * Dh, (h + 1) * Dh) for h in heads]

    qk_proj = [_dot(xc_b[:, sls[h]], wqk_ref[h]) for h in heads]
    q_b = [_bf(p[:, 0:Dh]) for p in qk_proj]
    k = [p[:, Dh:2 * Dh] * (Dh ** -0.5) for p in qk_proj]
    v1 = [jnp.concatenate([_bf(_dot(xm_b[:, sls[h]], wv_ref[h])), ones], axis=1) for h in heads]
    b_col = [col_f[:, h * 128:(h + 1) * 128] for h in heads]
    i_col = [col_i[:, h * 128:(h + 1) * 128] for h in heads]
    b_end = [b[L - 1:L, :] for b in b_col]
    m_st = [m_ref[h:h + 1, :] for h in heads]
    cn_st = [cn_ref[h] for h in heads]

    d_log = [jnp.where(causal, b_col[h] - bcum_t[SM_F + h:SM_F + h + 1, :] + gates_t[SM_I + h:SM_I + h + 1, :],
                       -jnp.inf) for h in heads]
    inter_log = [b_col[h] + m_st[h] for h in heads]
    m_t = [jnp.maximum(inter_log[h], jnp.max(d_log[h], axis=1, keepdims=True)) for h in heads]
    qk = [_bf(_dot_nt(q_b[h], _bf(k[h])) * jnp.exp(d_log[h] - m_t[h])) for h in heads]
    inter = [_dot(q_b[h], _bf(cn_st[h])) for h in heads]
    outs = []
    for h in heads:
        nd = _dot(qk[h], v1[h]) + inter[h] * twice(jnp.exp(inter_log[h] - m_t[h]))
        hh = nd[:, 0:Dh] / jnp.maximum(jnp.abs(nd[:, Dh:2 * Dh]), jnp.exp(-m_t[h]))
        hh = hh * jax.nn.sigmoid(o_pre[:, sls[h]])
        outs.append(_rms(hh, ng_ref[:, sls[h]]))
    y_ref[0] = jnp.concatenate(outs, axis=1)

    for h in heads:
        end_log = b_end[h] - b_col[h] + i_col[h]
        m_loc = jnp.max(end_log, axis=0, keepdims=True)
        kw = k[h] * jnp.exp(end_log - m_loc)
        cn_loc = _dot_tn(_bf(kw), v1[h])
        m_new = jnp.maximum(b_end[h] + m_st[h], m_loc)
        s_old = jnp.exp(b_end[h] + m_st[h] - m_new)
        s_new = jnp.exp(m_loc - m_new)
        cn_ref[h] = cn_st[h] * twice(s_old) + cn_loc * twice(s_new)
        m_ref[h:h + 1, :] = m_new


def _mlstm(u_ml, u_sm, conv_w, conv_b, wqk, wv, gb_pad, norm_g, tril3, selcol):
    bsz, t, _ = u_ml.shape
    nc = t // MLSTM_CHUNK
    const2 = lambda b, c: (0, 0)
    const3 = lambda b, c: (0, 0, 0)
    H, Dh = MLSTM_HEADS, MLSTM_HEAD_DIM
    return pl.pallas_call(
        _mlstm_kernel,
        grid=(bsz, nc),
        in_specs=[
            pl.BlockSpec((1, MLSTM_CHUNK, U_ML_W), lambda b, c: (b, c, 0)),
            pl.BlockSpec((1, MLSTM_CHUNK, U_SM_W), lambda b, c: (b, c, 0)),
            pl.BlockSpec((CONV_K, MLSTM_WIDTH), const2),
            pl.BlockSpec((1, MLSTM_WIDTH), const2),
            pl.BlockSpec((H, Dh, 2 * Dh), const3),
            pl.BlockSpec((H, Dh, Dh), const3),
            pl.BlockSpec((1, U_SM_W), const2),
            pl.BlockSpec((1, MLSTM_WIDTH), const2),
            pl.BlockSpec(tril3.shape, const2),
            pl.BlockSpec(selcol.shape, const2),
        ],
        out_specs=pl.BlockSpec((1, MLSTM_CHUNK, MLSTM_WIDTH), lambda b, c: (b, c, 0)),
        out_shape=jax.ShapeDtypeStruct((bsz, t, MLSTM_WIDTH), jnp.float32),
        scratch_shapes=[pltpu.VMEM((HALO + MLSTM_CHUNK, MLSTM_WIDTH), jnp.float32),
                        pltpu.VMEM((H, Dh, 2 * Dh), jnp.float32),
                        pltpu.VMEM((8, 128), jnp.float32)],
        compiler_params=_params("parallel", "arbitrary"),
        name="mlstm",
    )(u_ml, u_sm, conv_w, conv_b, wqk, wv, gb_pad, norm_g, tril3, selcol)


def _unit_lower_inverse(a_list, idx_r, idx_c, n):
    eye = (idx_r == idx_c).astype(jnp.float32)
    base = (idx_r >> 1) == (idx_c >> 1)
    ts = [eye + jnp.where(base, a, 0.0) for a in a_list]
    log_b = 1
    while (1 << log_b) < n:
        join = ((idx_r >> (log_b + 1)) == (idx_c >> (log_b + 1))) & ((idx_r >> log_b) != (idx_c >> log_b))
        tbs = [_bf(t) for t in ts]
        tes = [_bf(_dot(tb, _bf(jnp.where(join, a, 0.0)))) for tb, a in zip(tbs, a_list)]
        ts = [t + _dot(te, tb) for t, te, tb in zip(ts, tes, tbs)]
        log_b += 1
    return ts


def _split3(x):
    hi = _bf(x)
    r1 = x - hi.astype(jnp.float32)
    mid = _bf(r1)
    lo = _bf(r1 - mid.astype(jnp.float32))
    return hi, mid, lo


def _head_sum(x, hsum_bf):
    rows = x.shape[0]
    hi = _bf(x)
    lo = _bf(x - hi.astype(jnp.float32))
    s = _dot(jnp.concatenate([hi, lo], axis=0), hsum_bf)
    return s[0:rows] + s[rows:2 * rows]


def _rwkv_kernel(u_ref, mu_ref, w0_ref, wup_ref, a0_ref, aup_ref, gup_ref, kk_ref, ka_ref, rk_ref,
                 lnw_ref, lnb_ref, tril_ref, hsum_ref, y_ref, ext_ref, s_ref):
    C, H, N, W = RWKV_CHUNK, RWKV_HEADS, RWKV_HEAD_DIM, RWKV_WIDTH
    nb = u_ref.shape[0]

    @pl.when(pl.program_id(0) == 0)
    def _():
        ext_ref[:, 0:HALO, :] = jnp.zeros((nb, HALO, U_RW_W), jnp.float32)
        s_ref[...] = jnp.zeros_like(s_ref)

    u_rows, prev_rows = [], []
    for b in range(nb):
        ub = u_ref[b]
        ext_ref[b, HALO:HALO + C, :] = ub
        prev_rows.append(ext_ref[b, pl.ds(HALO - 1, C), :])
        ext_ref[b, 0:HALO, :] = ub[C - HALO:C, :]
        u_rows.append(ub)
    u = jnp.concatenate(u_rows, axis=0)
    u_prev = jnp.concatenate(prev_rows, axis=0)
    us = u + (u_prev - u) * mu_ref[...]
    r = us[:, 0:W]
    k = us[:, W:2 * W]
    v = us[:, 2 * W:3 * W]
    wa_l = us[:, 3 * W:3 * W + 128]
    g_l = us[:, 3 * W + 128:3 * W + 256]
    w_log = -_softplus(-(w0_ref[...] + _dot(_bf(jnp.tanh(wa_l)), wup_ref[...]))) - 0.5
    ld = -jnp.exp(w_log)
    a = jax.nn.sigmoid(a0_ref[...] + _dot(_bf(wa_l), aup_ref[...]))
    g = _dot(_bf(jax.nn.sigmoid(g_l)), gup_ref[...])
    hsum = hsum_ref[...]
    kk = k * kk_ref[...]
    kk = kk / jnp.maximum(jnp.sqrt(_head_sum(kk * kk, hsum)), 1e-6)
    k = k * (1.0 + (a - 1.0) * ka_ref[...])

    ld_parts = _split3(ld)
    cl3 = _dot(tril_ref[...], jnp.concatenate(ld_parts, axis=1))
    cl = cl3[:, 0:W] + cl3[:, W:2 * W] + cl3[:, 2 * W:3 * W]
    cl_end = jnp.concatenate([jnp.broadcast_to(cl[(b + 1) * C - 1:(b + 1) * C, :], (C, W)) for b in range(nb)],
                             axis=0)
    p_inv = jnp.exp(-cl)
    p_end = jnp.exp(cl_end - cl)
    a_t = -kk * jnp.exp(cl - ld)
    kka = kk * a
    b_t = kka * p_inv
    k_t = k * p_inv
    r_t = r * jnp.exp(cl)
    b_e = kka * p_end
    k_e = k * p_end

    PW = 2 * N
    idx_r = lax.broadcasted_iota(jnp.int32, (PW, PW), 0)
    idx_c = lax.broadcasted_iota(jnp.int32, (PW, PW), 1)
    same_head = (idx_r >> 6) == (idx_c >> 6)
    strict = same_head & (idx_r > idx_c)
    incl = same_head & (idx_r >= idx_c)
    lane = lax.broadcasted_iota(jnp.int32, (C, PW), 1)
    head0 = lane < N
    ones_c = jnp.ones((C, PW), jnp.bfloat16)

    def stack(x):
        return _bf(jnp.concatenate([jnp.where(head0, x, 0.0), jnp.where(head0, 0.0, x)], axis=0))

    probs = [(b, p) for b in range(nb) for p in range(H // 2)]
    pairs = range(len(probs))
    cut = lambda x, b, p: x[b * C:(b + 1) * C, p * PW:(p + 1) * PW]
    sts =[s_ref[q] for q in pairs]
    at_s = [stack(cut(a_t, b, p)) for b, p in probs]
    rt_s = [stack(cut(r_t, b, p)) for b, p in probs]
    v_s = [stack(cut(v, b, p)) for b, p in probs]
    aas = []
    for q, (b, p) in zip(pairs, probs):
        bt, kt = _bf(cut(b_t, b, p)), _bf(cut(k_t, b, p))
        aas.append(_dot_nt(jnp.concatenate([at_s[q], rt_s[q]], axis=0), jnp.concatenate([bt, bt, kt, kt], axis=0)))
    t_inv = _unit_lower_inverse([jnp.where(strict, aa[0:PW, 0:PW], 0.0) for aa in aas], idx_r, idx_c, C)
    rhs = [_dot(jnp.concatenate([at_s[p], _bf(jnp.where(strict, aas[p][0:PW, PW:2 * PW], 0.0))], axis=1),
                jnp.concatenate([_bf(sts[p]), v_s[p]], axis=0)) for p in pairs]
    uu = [_bf(_dot(_bf(t_inv[p]), _bf(rhs[p]))) for p in pairs]
    y_parts = []
    for p in pairs:
        a_rb = jnp.where(incl, aas[p][PW:2 * PW, 0:PW], 0.0)
        a_rk = jnp.where(incl, aas[p][PW:2 * PW, PW:2 * PW], 0.0)
        yy = _dot(jnp.concatenate([rt_s[p], _bf(a_rb), _bf(a_rk)], axis=1),
                  jnp.concatenate([_bf(sts[p]), uu[p], v_s[p]], axis=0))
        y_parts.append(yy[0:C] + yy[C:2 * C])
    for q, (b, p) in zip(pairs, probs):
        ld_end_col = sum(_dot_tn(cut(part, b, p), ones_c) for part in ld_parts)
        s_ref[q] = sts[q] * jnp.exp(ld_end_col) + _dot_tn(
            jnp.concatenate([stack(cut(b_e, b, p)), stack(cut(k_e, b, p))], axis=0),
            jnp.concatenate([uu[q], v_s[q]], axis=0))

    npair = H // 2
    y = jnp.concatenate([jnp.concatenate(y_parts[b * npair:(b + 1) * npair], axis=1) for b in range(nb)], axis=0)
    mu = _head_sum(y, hsum) * (1.0 / N)
    yc = y - mu
    var = _head_sum(yc * yc, hsum) * (1.0 / N)
    y = yc * lax.rsqrt(var + RWKV_GN_EPS) * lnw_ref[...] + lnb_ref[...]
    y = y + _head_sum(r * k * rk_ref[...], hsum) * v
    y = y * g
    for b in range(nb):
        y_ref[b] = y[b * C:(b + 1) * C]


def _rwkv(u_rw, mu, w0, wup_pad, a0, aup_pad, g_up, k_k, k_a, r_k, ln_w, ln_b, tril, hsum):
    bsz, t, _ = u_rw.shape
    nc = t // RWKV_CHUNK
    const = lambda c: (0, 0)
    vec = pl.BlockSpec((1, RWKV_WIDTH), const)
    lowrank = pl.BlockSpec((128, RWKV_WIDTH), const)
    return pl.pallas_call(
        _rwkv_kernel,
        grid=(nc,),
        in_specs=[
            pl.BlockSpec((bsz, RWKV_CHUNK, U_RW_W), lambda c: (0, c, 0)),
            pl.BlockSpec((1, U_RW_W), const),
            vec, lowrank, vec, lowrank, lowrank, vec, vec, vec, vec, vec,
            pl.BlockSpec((bsz * RWKV_CHUNK, bsz * RWKV_CHUNK), const),
            pl.BlockSpec((RWKV_WIDTH, RWKV_WIDTH), const),
        ],
        out_specs=pl.BlockSpec((bsz, RWKV_CHUNK, RWKV_WIDTH), lambda c: (0, c, 0)),
        out_shape=jax.ShapeDtypeStruct((bsz, t, RWKV_WIDTH), jnp.float32),
        scratch_shapes=[pltpu.VMEM((bsz, HALO + RWKV_CHUNK, U_RW_W), jnp.float32),
                        pltpu.VMEM((bsz * RWKV_HEADS // 2, 2 * RWKV_HEAD_DIM, 2 * RWKV_HEAD_DIM), jnp.float32)],
        compiler_params=_params("arbitrary"),
        name="rwkv",
    )(u_rw, mu, w0, wup_pad, a0, aup_pad, g_up, k_k, k_a, r_k, ln_w, ln_b, tril, hsum)


OUT_TM = 512


def _outproj_kernel(x_ref, ys_ref, ym_ref, yr_ref, ws_ref, wm_ref, wr_ref, o_ref):
    o_ref[...] = (x_ref[...] + _dot(_bf(ys_ref[...]), ws_ref[...]) + _dot(_bf(ym_ref[...]), wm_ref[...])
                  + _dot(_bf(yr_ref[...]), wr_ref[...]))


def _outproj(x2, y_ssd, y_ml, y_rw, w_s, w_m, w_r):
    n = x2.shape[0]
    tok = lambda w: pl.BlockSpec((OUT_TM, w), lambda i: (i, 0))
    wgt = lambda w: pl.BlockSpec((w, D_MODEL), lambda i: (0, 0))
    return pl.pallas_call(
        _outproj_kernel,
        grid=(n // OUT_TM,),
        in_specs=[tok(D_MODEL), tok(SSD_WIDTH), tok(MLSTM_WIDTH), tok(RWKV_WIDTH),
                  wgt(SSD_WIDTH), wgt(MLSTM_WIDTH), wgt(RWKV_WIDTH)],
        out_specs=tok(D_MODEL),
        out_shape=jax.ShapeDtypeStruct((n, D_MODEL), jnp.float32),
        compiler_params=_params("parallel"),
        name="outproj",
    )(x2, y_ssd, y_ml, y_rw, w_s, w_m, w_r)


def _pad_lanes(vec, offset, width=U_SM_W):
    out = jnp.zeros((1, width), jnp.float32)
    return out.at[0, offset:offset + vec.shape[0]].set(vec)


def kernel(x, ffn1_norm, ffn1_w_gate_up, ffn1_w_down, mix_norm, w_in, ssd_conv_w, ssd_conv_b, ssd_dt_bias, ssd_a_log, ssd_d, ssd_norm, mlstm_conv_w, mlstm_conv_b, mlstm_wq, mlstm_wk, mlstm_wv, mlstm_gate_bias, mlstm_norm, rwkv_shift_mu, rwkv_w0, rwkv_w_up, rwkv_a0, rwkv_a_up, rwkv_g_up, rwkv_k_k, rwkv_k_a, rwkv_r_k, rwkv_ln_w, rwkv_ln_b, w_out, ffn2_norm, ffn2_w_gate_up, ffn2_w_down, final_norm):
    bsz, t, d = x.shape
    depth = w_in.shape[0]
    n = bsz * t
    f32 = jnp.float32
    row = lambda v: v.reshape(1, -1).astype(f32)

    tril_l = jnp.tril(jnp.ones((SSD_CHUNK, SSD_CHUNK), f32))
    tril3 = _bf(jnp.concatenate([tril_l] * 3, axis=1))
    lane_head = jnp.arange(SSD_WIDTH) // SSD_HEAD_DIM
    expand1 = (jnp.arange(U_SM_W)[:, None] == lane_head[None, :]).astype(f32)
    expand = _bf(jnp.concatenate([expand1] * 2, axis=0))
    tile_head = jnp.arange(SSD_HEADS * 128) // 128
    selcol1 = (jnp.arange(U_SM_W)[:, None] == tile_head[None, :]).astype(f32)
    selcol = _bf(jnp.concatenate([selcol1] * 2, axis=0))
    ml_src = jnp.concatenate([SM_F + jnp.arange(MLSTM_HEADS), SM_I + jnp.arange(MLSTM_HEADS)])
    ml_sel1 = (jnp.arange(U_SM_W)[:, None] == jnp.repeat(ml_src, 128)[None, :]).astype(f32)
    ml_selcol = _bf(jnp.concatenate([ml_sel1] * 2, axis=0))
    tril_c = _bf(jnp.kron(jnp.eye(bsz, dtype=f32), jnp.tril(jnp.ones((RWKV_CHUNK, RWKV_CHUNK), f32))))
    head_id = jnp.arange(RWKV_WIDTH) // RWKV_HEAD_DIM
    hsum = _bf(head_id[:, None] == head_id[None, :])
    final_g = row(final_norm)

    x2 = x.reshape(n, d)
    for l in range(depth):
        wl = w_in[l]
        o_ml = D_SSD_IN
        o_rw = D_SSD_IN + D_MLSTM_IN
        w_ssd = _bf(wl[:, 0:U_SSD_W])
        w_ml = _bf(wl[:, o_ml:o_ml + U_ML_W])
        w_rw = _bf(wl[:, o_rw:o_rw + U_RW_W])
        w_sm = _bf(jnp.concatenate(
            [wl[:, U_SSD_W:D_SSD_IN], wl[:, o_ml + U_ML_W:o_rw],
             jnp.zeros((d, U_SM_W - SSD_HEADS - 2 * MLSTM_HEADS), f32)], axis=1))

        x2 = _ffn(x2, row(ffn1_norm[l]), _bf(ffn1_w_gate_up[l]), _bf(ffn1_w_down[l]), final_g, False)

        u_ssd, u_ml, u_rw, u_sm = _inproj(x2, row(mix_norm[l]), w_ssd, w_ml, w_rw, w_sm)
        u_sm3 = u_sm.reshape(bsz, t, U_SM_W)

        y_ssd = _ssd(
            u_ssd.reshape(bsz, t, U_SSD_W), u_sm3, ssd_conv_w[l], row(ssd_conv_b[l]),
            _pad_lanes(ssd_dt_bias[l], SM_DT), _pad_lanes(-jnp.exp(ssd_a_log[l]), SM_DT),
            row(jnp.repeat(ssd_d[l], SSD_HEAD_DIM)), row(ssd_norm[l]), tril3, expand, selcol)

        y_ml = _mlstm(
            u_ml.reshape(bsz, t, U_ML_W), u_sm3, mlstm_conv_w[l], row(mlstm_conv_b[l]),
            _bf(jnp.concatenate([mlstm_wq[l], mlstm_wk[l]], axis=-1)), _bf(mlstm_wv[l]),
            _pad_lanes(mlstm_gate_bias[l], SM_I), row(mlstm_norm[l]), tril3, ml_selcol)

        zeros64 = jnp.zeros((64, RWKV_WIDTH), f32)
        y_rw = _rwkv(
            u_rw.reshape(bsz, t, U_RW_W), row(rwkv_shift_mu[l]), row(rwkv_w0[l]),
            _bf(jnp.concatenate([rwkv_w_up[l], zeros64], axis=0)), row(rwkv_a0[l]),
            _bf(jnp.concatenate([zeros64, rwkv_a_up[l]], axis=0)), _bf(rwkv_g_up[l]),
            row(rwkv_k_k[l]), row(rwkv_k_a[l]), row(rwkv_r_k[l]), row(rwkv_ln_w[l]), row(rwkv_ln_b[l]),
            tril_c, hsum)

        wo = w_out[l]
        x2 = _outproj(x2, y_ssd.reshape(n, SSD_WIDTH), y_ml.reshape(n, MLSTM_WIDTH), y_rw.reshape(n, RWKV_WIDTH),
                      _bf(wo[0:SSD_WIDTH]), _bf(wo[SSD_WIDTH:SSD_WIDTH + MLSTM_WIDTH]),
                      _bf(wo[SSD_WIDTH + MLSTM_WIDTH:]))

        x2 = _ffn(x2, row(ffn2_norm[l]), _bf(ffn2_w_gate_up[l]), _bf(ffn2_w_down[l]), final_g, l == depth - 1)
    return x2.reshape(bsz, t, d)
```

```python
import functools

import jax
import jax.numpy as jnp
from jax import lax
from jax.experimental import pallas as pl
from jax.experimental.pallas import tpu as pltpu

D_MODEL = 1024
D_FF = 2816
SSD_WIDTH = 1024
SSD_HEAD_DIM = 64
SSD_HEADS = 16
SSD_GROUPS = 2
SSD_STATE = 64
SSD_CHUNK = 128
D_XBC = SSD_WIDTH + 2 * SSD_GROUPS * SSD_STATE
MLSTM_WIDTH = 512
MLSTM_HEADS = 4
MLSTM_HEAD_DIM = 128
MLSTM_CHUNK = 128
RWKV_WIDTH = 512
RWKV_HEADS = 8
RWKV_HEAD_DIM = 64
RWKV_CHUNK = 64
RWKV_SUB = 2
CONV_K = 4
EPS = 1e-6
RWKV_GN_EPS = 64e-5

D_SSD_IN = SSD_WIDTH + D_XBC + SSD_HEADS
D_MLSTM_IN = 2 * MLSTM_WIDTH + 2 * MLSTM_HEADS
D_RWKV_IN = 3 * RWKV_WIDTH + 64 + 64 + 128

U_SSD_W = SSD_WIDTH + D_XBC
U_ML_W = 2 * MLSTM_WIDTH
U_RW_W = D_RWKV_IN
U_SM_W = 128
SM_DT = 0
SM_I = SSD_HEADS
SM_F = SSD_HEADS + MLSTM_HEADS

HALO = 8
V7X_VMEM_LIMIT = 56 * 1024 * 1024
HIGHEST = lax.Precision.HIGHEST


def _silu(x):
    return x * jax.nn.sigmoid(x)


def _softplus(x):
    return jnp.maximum(x, 0.0) + jnp.log1p(jnp.exp(-jnp.abs(x)))


def _log_sigmoid(x):
    return -_softplus(-x)


def _rms(x, g):
    return x * lax.rsqrt(jnp.mean(x * x, axis=-1, keepdims=True) + EPS) * g


def _dot(a, b):
    return jnp.dot(a, b, preferred_element_type=jnp.float32)


def _dot_hi(a, b):
    return jnp.dot(a, b, preferred_element_type=jnp.float32, precision=HIGHEST)


def _dot_nt(a, b):
    return lax.dot_general(a, b, (((1,), (1,)), ((), ())), preferred_element_type=jnp.float32)


def _dot_tn(a, b):
    return lax.dot_general(a, b, (((0,), (0,)), ((), ())), preferred_element_type=jnp.float32)


def _bf(x):
    return x.astype(jnp.bfloat16)


def _params(*sem):
    return pltpu.CompilerParams(dimension_semantics=sem, vmem_limit_bytes=V7X_VMEM_LIMIT)


FFN_TM = 512
FFN_FC = 1408


def _resident(shape):
    return pl.BlockSpec(shape, lambda i: (0,) * len(shape), pipeline_mode=pl.Buffered(1))


def _ffn_half_step(x, g_ref, wgu_ref, wd_ref):
    h = _bf(_rms(x, g_ref[...]))
    acc = x
    for lo in range(0, D_FF, FFN_FC):
        gate = _dot(h, wgu_ref[:, lo:lo + FFN_FC])
        up = _dot(h, wgu_ref[:, D_FF + lo:D_FF + lo + FFN_FC])
        acc = acc + 0.5 * _dot(_bf(_silu(gate) * up), wd_ref[lo:lo + FFN_FC, :])
    return acc


def _ffn_kernel(x_ref, g_ref, wgu_ref, wd_ref, o_ref):
    o_ref[...] = _ffn_half_step(x_ref[...], g_ref, wgu_ref, wd_ref)


def _ffn(x2, g, w_gate_up, w_down):
    n = x2.shape[0]
    tok = pl.BlockSpec((FFN_TM, D_MODEL), lambda i: (i, 0))
    return pl.pallas_call(
        _ffn_kernel,
        grid=(n // FFN_TM,),
        in_specs=[tok, _resident((1, D_MODEL)), _resident((D_MODEL, 2 * D_FF)), _resident((D_FF, D_MODEL))],
        out_specs=tok,
        out_shape=jax.ShapeDtypeStruct((n, D_MODEL), jnp.float32),
        compiler_params=_params("parallel"),
        name="ffn",
    )(x2, g, w_gate_up, w_down)


def _mix_ffn_kernel(x_ref, ys_ref, ym_ref, yr_ref, ws_ref, wm_ref, wr_ref, g_ref, wgu_ref, wd_ref, fg_ref,
                    o_ref, *, final_norm):
    x = (x_ref[...] + _dot(ys_ref[...], ws_ref[...]) + _dot(ym_ref[...], wm_ref[...])
         + _dot(yr_ref[...], wr_ref[...]))
    y = _ffn_half_step(x, g_ref, wgu_ref, wd_ref)
    if final_norm:
        y = _rms(y, fg_ref[...])
    o_ref[...] = y


def _mix_ffn(x2, y_ssd, y_ml, y_rw, w_s, w_m, w_r, g, w_gate_up, w_down, final_g, final_norm):
    n = x2.shape[0]
    tok = lambda w: pl.BlockSpec((FFN_TM, w), lambda i: (i, 0))
    return pl.pallas_call(
        functools.partial(_mix_ffn_kernel, final_norm=final_norm),
        grid=(n // FFN_TM,),
        in_specs=[tok(D_MODEL), tok(SSD_WIDTH), tok(MLSTM_WIDTH), tok(RWKV_WIDTH),
                  _resident((SSD_WIDTH, D_MODEL)), _resident((MLSTM_WIDTH, D_MODEL)),
                  _resident((RWKV_WIDTH, D_MODEL)), _resident((1, D_MODEL)),
                  _resident((D_MODEL, 2 * D_FF)), _resident((D_FF, D_MODEL)), _resident((1, D_MODEL))],
        out_specs=tok(D_MODEL),
        out_shape=jax.ShapeDtypeStruct((n, D_MODEL), jnp.float32),
        compiler_params=_params("parallel"),
        name="mix_ffn",
    )(x2, y_ssd, y_ml, y_rw, w_s, w_m, w_r, g, w_gate_up, w_down, final_g)


INP_TM = 512


def _inproj_kernel(x_ref, g_ref, wssd_ref, wml_ref, wrw_ref, wsm_ref, ossd_ref, oml_ref, orw_ref, osm_ref):
    h = _bf(_rms(x_ref[...], g_ref[...]))
    ossd_ref[...] = _dot(h, wssd_ref[...])
    oml_ref[...] = _dot(h, wml_ref[...])
    orw_ref[...] = _dot(h, wrw_ref[...])
    osm_ref[...] = _dot(h, wsm_ref[...])


def _inproj(x2, g, w_ssd, w_ml, w_rw, w_sm):
    n = x2.shape[0]
    widths = (U_SSD_W, U_ML_W, U_RW_W, U_SM_W)
    return pl.pallas_call(
        _inproj_kernel,
        grid=(n // INP_TM,),
        in_specs=[pl.BlockSpec((INP_TM, D_MODEL), lambda i: (i, 0)), _resident((1, D_MODEL))]
                 + [_resident((D_MODEL, w)) for w in widths],
        out_specs=[pl.BlockSpec((INP_TM, w), lambda i: (i, 0)) for w in widths],
        out_shape=[jax.ShapeDtypeStruct((n, w), jnp.float32) for w in widths],
        compiler_params=_params("parallel"),
        name="inproj",
    )(x2, g, w_ssd, w_ml, w_rw, w_sm)


def _causal_conv(ext_ref, cur, w_ref, b_ref, first):
    rows = cur.shape[0]

    @pl.when(first)
    def _():
        ext_ref[0:HALO, :] = jnp.zeros((HALO, cur.shape[1]), jnp.float32)

    ext_ref[HALO:HALO + rows, :] = cur
    acc = b_ref[...] + cur * w_ref[CONV_K - 1:CONV_K, :]
    for j in range(CONV_K - 1):
        acc = acc + ext_ref[pl.ds(HALO - (CONV_K - 1) + j, rows), :] * w_ref[j:j + 1, :]
    ext_ref[0:HALO, :] = cur[rows - HALO:rows, :]
    return acc


def _split_lanes(x, terms):
    parts, rest = [], x
    for i in range(terms):
        p = _bf(rest)
        parts.append(p)
        if i + 1 < terms:
            rest = rest - p.astype(jnp.float32)
    return jnp.concatenate(parts, axis=1)


def _split_rows(x, terms):
    parts, rest = [], x
    for i in range(terms):
        p = _bf(rest)
        parts.append(p)
        if i + 1 < terms:
            rest = rest - p.astype(jnp.float32)
    return jnp.concatenate(parts, axis=0)


def _ssd_kernel(u_ref, sm_ref, cw_ref, cb_ref, dtb_ref, nega_ref, dskip_ref, ng_ref, tril_ref, expand_ref,
                selcol_ref, y_ref, ext_ref, st_ref):
    L, P, N, G = SSD_CHUNK, SSD_HEAD_DIM, SSD_STATE, SSD_GROUPS
    H = SSD_HEADS
    first = pl.program_id(1) == 0

    @pl.when(first)
    def _():
        st_ref[...] = jnp.zeros_like(st_ref)

    z = u_ref[0, :, 0:SSD_WIDTH]
    xbc = _silu(_causal_conv(ext_ref, u_ref[0, :, SSD_WIDTH:U_SSD_W], cw_ref, cb_ref, first))
    xs = xbc[:, 0:SSD_WIDTH]
    b_in = xbc[:, SSD_WIDTH:SSD_WIDTH + G * N]
    c_in = xbc[:, SSD_WIDTH + G * N:SSD_WIDTH + 2 * G * N]
    dt = _softplus(sm_ref[0] + dtb_ref[...])
    a = dt * nega_ref[...]
    a_cum = _dot(tril_ref[...], _split_rows(a, 3))
    a_cum_t = a_cum.T
    a_cum2 = _split_lanes(a_cum, 2)
    a_wide = _dot(a_cum2, expand_ref[...])
    col_b = _dot(a_cum2, selcol_ref[...])
    dt_wide = _dot(_split_lanes(dt, 2), expand_ref[...])
    exp_a = jnp.exp(a_wide)
    x_dt = xs * dt_wide
    xdd = x_dt * jnp.exp(a_wide[L - 1:L, :] - a_wide)

    row = lax.broadcasted_iota(jnp.int32, (L, L), 0)
    col = lax.broadcasted_iota(jnp.int32, (L, L), 1)
    causal = row >= col
    group0 = col < N
    c_b = _bf(c_in)
    b_b = _bf(b_in)
    cbs = [_dot_nt(_bf(jnp.where(group0 if g == 0 else ~group0, c_in, 0.0)), b_b) for g in range(G)]
    st = st_ref[...]
    y_off = _dot(c_b, _bf(st))

    x_dt_b = _bf(x_dt)
    y_parts = []
    for hp in range(H // 2):
        cb_g = cbs[(2 * hp) // (H // G)]
        x_pair = x_dt_b[:, hp * 2 * P:(hp + 1) * 2 * P]
        halves = []
        for h in (2 * hp, 2 * hp + 1):
            seg = col_b[:, h * 128:(h + 1) * 128] - a_cum_t[h:h + 1, :]
            halves.append(_dot(_bf(cb_g * jnp.exp(jnp.where(causal, seg, -jnp.inf))), x_pair))
        y_parts.append(jnp.where(group0, halves[0], halves[1]))
    y = jnp.concatenate(y_parts, axis=1) + y_off * exp_a + xs * dskip_ref[...]

    srow = lax.broadcasted_iota(jnp.int32, (G * N, H * P), 0)
    scol = lax.broadcasted_iota(jnp.int32, (G * N, H * P), 1)
    own = (srow >= N) == (scol >= (H // G) * P)
    st_ref[...] = st * exp_a[L - 1:L, :] + jnp.where(own, _dot_tn(b_b, _bf(xdd)), 0.0)

    y = y * _silu(z)
    gw = SSD_WIDTH // G
    y_ref[0] = _bf(jnp.concatenate(
        [_rms(y[:, g * gw:(g + 1) * gw], ng_ref[:, g * gw:(g + 1) * gw]) for g in range(G)], axis=1))


def _ssd(u_ssd, u_sm, conv_w, conv_b, dtb_pad, nega_pad, dskip_wide, norm_g, tril3, expand, selcol):
    bsz, t, _ = u_ssd.shape
    nc = t // SSD_CHUNK
    const = lambda b, c: (0, 0)
    return pl.pallas_call(
        _ssd_kernel,
        grid=(bsz, nc),
        in_specs=[
            pl.BlockSpec((1, SSD_CHUNK, U_SSD_W), lambda b, c: (b, c, 0)),
            pl.BlockSpec((1, SSD_CHUNK, U_SM_W), lambda b, c: (b, c, 0)),
            pl.BlockSpec((CONV_K, D_XBC), const),
            pl.BlockSpec((1, D_XBC), const),
            pl.BlockSpec((1, U_SM_W), const),
            pl.BlockSpec((1, U_SM_W), const),
            pl.BlockSpec((1, SSD_WIDTH), const),
            pl.BlockSpec((1, SSD_WIDTH), const),
            pl.BlockSpec(tril3.shape, const),
            pl.BlockSpec(expand.shape, const),
            pl.BlockSpec(selcol.shape, const),
        ],
        out_specs=pl.BlockSpec((1, SSD_CHUNK, SSD_WIDTH), lambda b, c: (b, c, 0)),
        out_shape=jax.ShapeDtypeStruct((bsz, t, SSD_WIDTH), jnp.bfloat16),
        scratch_shapes=[pltpu.VMEM((HALO + SSD_CHUNK, D_XBC), jnp.float32),
                        pltpu.VMEM((SSD_GROUPS * SSD_STATE, SSD_WIDTH), jnp.float32)],
        compiler_params=_params("parallel", "arbitrary"),
        name="ssd",
    )(u_ssd, u_sm, conv_w, conv_b, dtb_pad, nega_pad, dskip_wide, norm_g, tril3, expand, selcol)


def _mlstm_kernel(u_ref, sm_ref, cw_ref, cb_ref, wqk_ref, wv_ref, gb_ref, ng_ref, tril_ref, selcol_ref,
                  y_ref, ext_ref, cn_ref, m_ref):
    L, H, Dh = MLSTM_CHUNK, MLSTM_HEADS, MLSTM_HEAD_DIM
    first = pl.program_id(1) == 0

    @pl.when(first)
    def _():
        cn_ref[...] = jnp.zeros_like(cn_ref)
        m_ref[...] = jnp.zeros_like(m_ref)

    xm = u_ref[0, :, 0:MLSTM_WIDTH]
    o_pre = u_ref[0, :, MLSTM_WIDTH:U_ML_W]
    xc_b = _bf(_silu(_causal_conv(ext_ref, xm, cw_ref, cb_ref, first)))
    xm_b = _bf(xm)
    gates = sm_ref[0] + gb_ref[...]
    log_f = _log_sigmoid(gates)
    bcum = _dot(tril_ref[...], _split_rows(log_f, 3))
    bcum_t = bcum.T
    gates_t = gates.T
    col_f = _dot(_split_lanes(bcum, 2), selcol_ref[:, 0:H * 128])
    col_i = _dot(_split_lanes(gates, 2), selcol_ref[:, H * 128:2 * H * 128])
    row = lax.broadcasted_iota(jnp.int32, (L, L), 0)
    col = lax.broadcasted_iota(jnp.int32, (L, L), 1)
    causal = row >= col
    ones = jnp.ones((L, Dh), jnp.bfloat16)
    twice = lambda x: jnp.concatenate([x, x], axis=1)
    heads = range(H)
    sls = [slice(h * Dh, (h + 1) * Dh) for h in heads]

    qk_proj = [_dot(xc_b[:, sls[h]], wqk_ref[h]) for h in heads]
    q_b = [_bf(p[:, 0:Dh]) for p in qk_proj]
    k = [p[:, Dh:2 * Dh] * (Dh ** -0.5) for p in qk_proj]
    v1 = [jnp.concatenate([_bf(_dot(xm_b[:, sls[h]], wv_ref[h])), ones], axis=1) for h in heads]
    b_col = [col_f[:, h * 128:(h + 1) * 128] for h in heads]
    i_col = [col_i[:, h * 128:(h + 1) * 128] for h in heads]
    b_end = [b[L - 1:L, :] for b in b_col]
    m_st = [m_ref[h:h + 1, :] for h in heads]
    cn_st = [cn_ref[h] for h in heads]

    d_log = [jnp.where(causal, b_col[h] - bcum_t[SM_F + h:SM_F + h + 1, :] + gates_t[SM_I + h:SM_I + h + 1, :],
                       -jnp.inf) for h in heads]
    inter_log = [b_col[h] + m_st[h] for h in heads]
    m_t = [jnp.maximum(inter_log[h], jnp.max(d_log[h], axis=1, keepdims=True)) for h in heads]
    qk = [_bf(_dot_nt(q_b[h], _bf(k[h])) * jnp.exp(d_log[h] - m_t[h])) for h in heads]
    inter = [_dot(q_b[h], _bf(cn_st[h])) for h in heads]
    outs = []
    for h in heads:
        nd = _dot(qk[h], v1[h]) + inter[h] * twice(jnp.exp(inter_log[h] - m_t[h]))
        hh = nd[:, 0:Dh] / jnp.maximum(jnp.abs(nd[:, Dh:2 * Dh]), jnp.exp(-m_t[h]))
        hh = hh * jax.nn.sigmoid(o_pre[:, sls[h]])
        outs.append(_rms(hh, ng_ref[:, sls[h]]))
    y_ref[0] = _bf(jnp.concatenate(outs, axis=1))

    for h in heads:
        end_log = b_end[h] - b_col[h] + i_col[h]
        m_loc = jnp.max(end_log, axis=0, keepdims=True)
        kw = k[h] * jnp.exp(end_log - m_loc)
        cn_loc = _dot_tn(_bf(kw), v1[h])
        m_new = jnp.maximum(b_end[h] + m_st[h], m_loc)
        s_old = jnp.exp(b_end[h] + m_st[h] - m_new)
        s_new = jnp.exp(m_loc - m_new)
        cn_ref[h] = cn_st[h] * twice(s_old) + cn_loc * twice(s_new)
        m_ref[h:h + 1, :] = m_new


def _mlstm(u_ml, u_sm, conv_w, conv_b, wqk, wv, gb_pad, norm_g, tril3, selcol):
    bsz, t, _ = u_ml.shape
    nc = t // MLSTM_CHUNK
    const2 = lambda b, c: (0, 0)
    const3 = lambda b, c: (0, 0, 0)
    H, Dh = MLSTM_HEADS, MLSTM_HEAD_DIM
    return pl.pallas_call(
        _mlstm_kernel,
        grid=(bsz, nc),
        in_specs=[
            pl.BlockSpec((1, MLSTM_CHUNK, U_ML_W), lambda b, c: (b, c, 0)),
            pl.BlockSpec((1, MLSTM_CHUNK, U_SM_W), lambda b, c: (b, c, 0)),
            pl.BlockSpec((CONV_K, MLSTM_WIDTH), const2),
            pl.BlockSpec((1, MLSTM_WIDTH), const2),
            pl.BlockSpec((H, Dh, 2 * Dh), const3),
            pl.BlockSpec((H, Dh, Dh), const3),
            pl.BlockSpec((1, U_SM_W), const2),
            pl.BlockSpec((1, MLSTM_WIDTH), const2),
            pl.BlockSpec(tril3.shape, const2),
            pl.BlockSpec(selcol.shape, const2),
        ],
        out_specs=pl.BlockSpec((1, MLSTM_CHUNK, MLSTM_WIDTH), lambda b, c: (b, c, 0)),
        out_shape=jax.ShapeDtypeStruct((bsz, t, MLSTM_WIDTH), jnp.bfloat16),
        scratch_shapes=[pltpu.VMEM((HALO + MLSTM_CHUNK, MLSTM_WIDTH), jnp.float32),
                        pltpu.VMEM((H, Dh, 2 * Dh), jnp.float32),
                        pltpu.VMEM((8, 128), jnp.float32)],
        compiler_params=_params("parallel", "arbitrary"),
        name="mlstm",
    )(u_ml, u_sm, conv_w, conv_b, wqk, wv, gb_pad, norm_g, tril3, selcol)


def _unit_lower_inverse(a_list, idx_r, idx_c, n):
    size = a_list[0].shape[0]
    eye = (idx_r == idx_c).astype(jnp.float32)
    base = (idx_r >> 1) == (idx_c >> 1)
    ts = [eye + jnp.where(base, a, 0.0) for a in a_list]
    log_b = 1
    while (1 << log_b) < n:
        b = 1 << log_b
        join = ((idx_r >> (log_b + 1)) == (idx_c >> (log_b + 1))) & ((idx_r >> log_b) != (idx_c >> log_b))
        es = [_bf(jnp.where(join, a, 0.0)) for a in a_list]
        tbs = [_bf(t) for t in ts]
        if b < 8:
            tes = [_bf(_dot(tb, e)) for tb, e in zip(tbs, es)]
            ts = [t + _dot(te, tb) for t, te, tb in zip(ts, tes, tbs)]
        else:
            lower = [slice(s + b, s + 2 * b) for s in range(0, size, 2 * b)]
            t_low = [jnp.concatenate([t[sl] for sl in lower], axis=0) for t in ts]
            tes = [_bf(_dot(_bf(tl), e)) for tl, e in zip(t_low, es)]
            upd = [tl + _dot(te, tb) for tl, te, tb in zip(t_low, tes, tbs)]
            ts = [jnp.concatenate([piece for j, sl in enumerate(lower)
                                   for piece in (t[sl.start - b:sl.start], u[j * b:(j + 1) * b])], axis=0)
                  for t, u in zip(ts, upd)]
        log_b += 1
    return ts


def _head_sum(x, hsum_bf):
    rows = x.shape[0]
    hi = _bf(x)
    lo = _bf(x - hi.astype(jnp.float32))
    s = _dot(jnp.concatenate([hi, lo], axis=0), hsum_bf)
    return s[0:rows] + s[rows:2 * rows]


def _rwkv_kernel(u_ref, mu_ref, w0_ref, wup_ref, a0_ref, aup_ref, gup_ref, kk_ref, ka_ref, rk_ref,
                 lnw_ref, lnb_ref, tril_ref, hsum_ref, y_ref, ext_ref, s_ref):
    C, H, N, W, SUB = RWKV_CHUNK, RWKV_HEADS, RWKV_HEAD_DIM, RWKV_WIDTH, RWKV_SUB
    nb = u_ref.shape[0]
    rows = SUB * C
    nseg = nb * SUB

    @pl.when(pl.program_id(0) == 0)
    def _():
        ext_ref[:, 0:HALO, :] = jnp.zeros((nb, HALO, U_RW_W), jnp.float32)
        s_ref[...] = jnp.zeros_like(s_ref)

    u_rows, prev_rows = [], []
    for b in range(nb):
        ub = u_ref[b]
        ext_ref[b, HALO:HALO + rows, :] = ub
        prev_rows.append(ext_ref[b, pl.ds(HALO - 1, rows), :])
        ext_ref[b, 0:HALO, :] = ub[rows - HALO:rows, :]
        u_rows.append(ub)
    u = jnp.concatenate(u_rows, axis=0)
    u_prev = jnp.concatenate(prev_rows, axis=0)
    us = u + (u_prev - u) * mu_ref[...]
    r = us[:, 0:W]
    k = us[:, W:2 * W]
    v = us[:, 2 * W:3 * W]
    wa_l = us[:, 3 * W:3 * W + 128]
    g_l = us[:, 3 * W + 128:3 * W + 256]
    w_log = -_softplus(-(w0_ref[...] + _dot(_bf(jnp.tanh(wa_l)), wup_ref[...]))) - 0.5
    ld = -jnp.exp(w_log)
    a = jax.nn.sigmoid(a0_ref[...] + _dot(_bf(wa_l), aup_ref[...]))
    g = _dot(_bf(jax.nn.sigmoid(g_l)), gup_ref[...])
    hsum = hsum_ref[...]
    kk = k * kk_ref[...]
    kk = kk / jnp.maximum(jnp.sqrt(_head_sum(kk * kk, hsum)), 1e-6)
    k = k * (1.0 + (a - 1.0) * ka_ref[...])

    cl = _dot(tril_ref[...], _split_rows(ld, 3))
    cl_end = jnp.concatenate([jnp.broadcast_to(cl[(s + 1) * C - 1:(s + 1) * C, :], (C, W)) for s in range(nseg)],
                             axis=0)
    p_inv = jnp.exp(-cl)
    p_end = jnp.exp(cl_end - cl)
    a_t = -kk * jnp.exp(cl - ld)
    kka = kk * a
    b_t = kka * p_inv
    k_t = k * p_inv
    r_t = r * jnp.exp(cl)
    b_e = kka * p_end
    k_e = k * p_end

    PW = 2 * N
    idx_r = lax.broadcasted_iota(jnp.int32, (PW, PW), 0)
    idx_c = lax.broadcasted_iota(jnp.int32, (PW, PW), 1)
    same_head = (idx_r >> 6) == (idx_c >> 6)
    strict = same_head & (idx_r > idx_c)
    incl = same_head & (idx_r >= idx_c)
    lane = lax.broadcasted_iota(jnp.int32, (C, PW), 1)
    head0 = lane < N

    def stack(x):
        return _bf(jnp.concatenate([jnp.where(head0, x, 0.0), jnp.where(head0, 0.0, x)], axis=0))

    npair = H // 2
    probs = [(s, p) for s in range(nseg) for p in range(npair)]
    cut = lambda x, s, p: x[s * C:(s + 1) * C, p * PW:(p + 1) * PW]
    at_s = [stack(cut(a_t, s, p)) for s, p in probs]
    rt_s = [stack(cut(r_t, s, p)) for s, p in probs]
    v_s = [stack(cut(v, s, p)) for s, p in probs]
    aas = []
    for q, (s, p) in enumerate(probs):
        bt, kt = _bf(cut(b_t, s, p)), _bf(cut(k_t, s, p))
        aas.append(_dot_nt(jnp.concatenate([at_s[q], rt_s[q]], axis=0), jnp.concatenate([bt, bt, kt, kt], axis=0)))
    t_inv = _unit_lower_inverse([jnp.where(strict, aa[0:PW, 0:PW], 0.0) for aa in aas], idx_r, idx_c, C)
    t_inv = [_bf(t) for t in t_inv]

    sts = [s_ref[i] for i in range(nb * npair)]
    y_parts = [None] * len(probs)
    for j in range(SUB):
        qs = [((b * SUB + j) * npair + p, b * npair + p) for b in range(nb) for p in range(npair)]
        rhs = [_dot(jnp.concatenate([at_s[q], _bf(jnp.where(strict, aas[q][0:PW, PW:2 * PW], 0.0))], axis=1),
                    jnp.concatenate([_bf(sts[i]), v_s[q]], axis=0)) for q, i in qs]
        uu = [_bf(_dot(t_inv[q], _bf(r_))) for (q, i), r_ in zip(qs, rhs)]
        for (q, i), u_ in zip(qs, uu):
            a_rb = jnp.where(incl, aas[q][PW:2 * PW, 0:PW], 0.0)
            a_rk = jnp.where(incl, aas[q][PW:2 * PW, PW:2 * PW], 0.0)
            yy = _dot(jnp.concatenate([rt_s[q], _bf(a_rb), _bf(a_rk)], axis=1),
                      jnp.concatenate([_bf(sts[i]), u_, v_s[q]], axis=0))
            y_parts[q] = yy[0:C] + yy[C:2 * C]
        for (q, i), u_ in zip(qs, uu):
            s, p = probs[q]
            ld_end_col = jnp.broadcast_to(cl[(s + 1) * C - 1:(s + 1) * C, p * PW:(p + 1) * PW], (PW, PW)).T
            sts[i] = sts[i] * jnp.exp(ld_end_col) + _dot_tn(
                jnp.concatenate([stack(cut(b_e, s, p)), stack(cut(k_e, s, p))], axis=0),
                jnp.concatenate([u_, v_s[q]], axis=0))
    for i, st in enumerate(sts):
        s_ref[i] = st

    y = jnp.concatenate([jnp.concatenate(y_parts[s * npair:(s + 1) * npair], axis=1) for s in range(nseg)], axis=0)
    mu = _head_sum(y, hsum) * (1.0 / N)
    yc = y - mu
    var = _head_sum(yc * yc, hsum) * (1.0 / N)
    y = yc * lax.rsqrt(var + RWKV_GN_EPS) * lnw_ref[...] + lnb_ref[...]
    y = y + _head_sum(r * k * rk_ref[...], hsum) * v
    y = y * g
    for b in range(nb):
        y_ref[b] = _bf(y[b * rows:(b + 1) * rows])


def _rwkv(u_rw, mu, w0, wup_pad, a0, aup_pad, g_up, k_k, k_a, r_k, ln_w, ln_b, tril, hsum):
    bsz, t, _ = u_rw.shape
    step = RWKV_SUB * RWKV_CHUNK
    const = lambda c: (0, 0)
    vec = pl.BlockSpec((1, RWKV_WIDTH), const)
    lowrank = pl.BlockSpec((128, RWKV_WIDTH), const)
    return pl.pallas_call(
        _rwkv_kernel,
        grid=(t // step,),
        in_specs=[
            pl.BlockSpec((bsz, step, U_RW_W), lambda c: (0, c, 0)),
            pl.BlockSpec((1, U_RW_W), const),
            vec, lowrank, vec, lowrank, lowrank, vec, vec, vec, vec, vec,
            pl.BlockSpec(tril.shape, const),
            pl.BlockSpec((RWKV_WIDTH, RWKV_WIDTH), const),
        ],
        out_specs=pl.BlockSpec((bsz, step, RWKV_WIDTH), lambda c: (0, c, 0)),
        out_shape=jax.ShapeDtypeStruct((bsz, t, RWKV_WIDTH), jnp.bfloat16),
        scratch_shapes=[pltpu.VMEM((bsz, HALO + step, U_RW_W), jnp.float32),
                        pltpu.VMEM((bsz * RWKV_HEADS // 2, 2 * RWKV_HEAD_DIM, 2 * RWKV_HEAD_DIM), jnp.float32)],
        compiler_params=_params("arbitrary"),
        name="rwkv",
    )(u_rw, mu, w0, wup_pad, a0, aup_pad, g_up, k_k, k_a, r_k, ln_w, ln_b, tril, hsum)


def _pad_lanes(vec, offset, width=U_SM_W):
    out = jnp.zeros((1, width), jnp.float32)
    return out.at[0, offset:offset + vec.shape[0]].set(vec)


def kernel(x, ffn1_norm, ffn1_w_gate_up, ffn1_w_down, mix_norm, w_in, ssd_conv_w, ssd_conv_b, ssd_dt_bias, ssd_a_log, ssd_d, ssd_norm, mlstm_conv_w, mlstm_conv_b, mlstm_wq, mlstm_wk, mlstm_wv, mlstm_gate_bias, mlstm_norm, rwkv_shift_mu, rwkv_w0, rwkv_w_up, rwkv_a0, rwkv_a_up, rwkv_g_up, rwkv_k_k, rwkv_k_a, rwkv_r_k, rwkv_ln_w, rwkv_ln_b, w_out, ffn2_norm, ffn2_w_gate_up, ffn2_w_down, final_norm):
    bsz, t, d = x.shape
    depth = w_in.shape[0]
    n = bsz * t
    f32 = jnp.float32
    row = lambda v: v.reshape(1, -1).astype(f32)

    tril_l = jnp.tril(jnp.ones((SSD_CHUNK, SSD_CHUNK), f32))
    tril3 = _bf(jnp.concatenate([tril_l] * 3, axis=1))
    lane_head = jnp.arange(SSD_WIDTH) // SSD_HEAD_DIM
    expand1 = (jnp.arange(U_SM_W)[:, None] == lane_head[None, :]).astype(f32)
    expand = _bf(jnp.concatenate([expand1] * 2, axis=0))
    tile_head = jnp.arange(SSD_HEADS * 128) // 128
    selcol1 = (jnp.arange(U_SM_W)[:, None] == tile_head[None, :]).astype(f32)
    selcol = _bf(jnp.concatenate([selcol1] * 2, axis=0))
    ml_src = jnp.concatenate([SM_F + jnp.arange(MLSTM_HEADS), SM_I + jnp.arange(MLSTM_HEADS)])
    ml_sel1 = (jnp.arange(U_SM_W)[:, None] == jnp.repeat(ml_src, 128)[None, :]).astype(f32)
    ml_selcol = _bf(jnp.concatenate([ml_sel1] * 2, axis=0))
    tril_c = jnp.kron(jnp.eye(bsz * RWKV_SUB, dtype=f32), jnp.tril(jnp.ones((RWKV_CHUNK, RWKV_CHUNK), f32)))
    tril_c = _bf(jnp.concatenate([tril_c] * 3, axis=1))
    head_id = jnp.arange(RWKV_WIDTH) // RWKV_HEAD_DIM
    hsum = _bf(head_id[:, None] == head_id[None, :])
    final_g = row(final_norm)

    x2 = x.reshape(n, d)
    for l in range(depth):
        wl = w_in[l]
        o_ml = D_SSD_IN
        o_rw = D_SSD_IN + D_MLSTM_IN
        w_ssd = _bf(wl[:, 0:U_SSD_W])
        w_ml = _bf(wl[:, o_ml:o_ml + U_ML_W])
        w_rw = _bf(wl[:, o_rw:o_rw + U_RW_W])
        w_sm = _bf(jnp.concatenate(
            [wl[:, U_SSD_W:D_SSD_IN], wl[:, o_ml + U_ML_W:o_rw],
             jnp.zeros((d, U_SM_W - SSD_HEADS - 2 * MLSTM_HEADS), f32)], axis=1))

        x2 = _ffn(x2, row(ffn1_norm[l]), _bf(ffn1_w_gate_up[l]), _bf(ffn1_w_down[l]))

        u_ssd, u_ml, u_rw, u_sm = _inproj(x2, row(mix_norm[l]), w_ssd, w_ml, w_rw, w_sm)
        u_sm3 = u_sm.reshape(bsz, t, U_SM_W)

        y_ssd = _ssd(
            u_ssd.reshape(bsz, t, U_SSD_W), u_sm3, ssd_conv_w[l], row(ssd_conv_b[l]),
            _pad_lanes(ssd_dt_bias[l], SM_DT), _pad_lanes(-jnp.exp(ssd_a_log[l]), SM_DT),
            row(jnp.repeat(ssd_d[l], SSD_HEAD_DIM)), row(ssd_norm[l]), tril3, expand, selcol)

        y_ml = _mlstm(
            u_ml.reshape(bsz, t, U_ML_W), u_sm3, mlstm_conv_w[l], row(mlstm_conv_b[l]),
            _bf(jnp.concatenate([mlstm_wq[l], mlstm_wk[l]], axis=-1)), _bf(mlstm_wv[l]),
            _pad_lanes(mlstm_gate_bias[l], SM_I), row(mlstm_norm[l]), tril3, ml_selcol)

        zeros64 = jnp.zeros((64, RWKV_WIDTH), f32)
        y_rw = _rwkv(
            u_rw.reshape(bsz, t, U_RW_W), row(rwkv_shift_mu[l]), row(rwkv_w0[l]),
            _bf(jnp.concatenate([rwkv_w_up[l], zeros64], axis=0)), row(rwkv_a0[l]),
            _bf(jnp.concatenate([zeros64, rwkv_a_up[l]], axis=0)), _bf(rwkv_g_up[l]),
            row(rwkv_k_k[l]), row(rwkv_k_a[l]), row(rwkv_r_k[l]), row(rwkv_ln_w[l]), row(rwkv_ln_b[l]),
            tril_c, hsum)

        wo = w_out[l]
        x2 = _mix_ffn(x2, y_ssd.reshape(n, SSD_WIDTH), y_ml.reshape(n, MLSTM_WIDTH), y_rw.reshape(n, RWKV_WIDTH),
                      _bf(wo[0:SSD_WIDTH]), _bf(wo[SSD_WIDTH:SSD_WIDTH + MLSTM_WIDTH]),
                      _bf(wo[SSD_WIDTH + MLSTM_WIDTH:]),
                      row(ffn2_norm[l]), _bf(ffn2_w_gate_up[l]), _bf(ffn2_w_down[l]), final_g, l == depth - 1)
    return x2.reshape(bsz, t, d)
```

```python
import functools

import jax
import jax.numpy as jnp
from jax import lax
from jax.experimental import pallas as pl
from jax.experimental.pallas import tpu as pltpu

D_MODEL = 1024
D_FF = 2816
SSD_WIDTH = 1024
SSD_HEAD_DIM = 64
SSD_HEADS = 16
SSD_GROUPS = 2
SSD_STATE = 64
SSD_CHUNK = 128
D_XBC = SSD_WIDTH + 2 * SSD_GROUPS * SSD_STATE
MLSTM_WIDTH = 512
MLSTM_HEADS = 4
MLSTM_HEAD_DIM = 128
MLSTM_CHUNK = 128
RWKV_WIDTH = 512
RWKV_HEADS = 8
RWKV_HEAD_DIM = 64
RWKV_CHUNK = 64
RWKV_SUB = 2
MIX_STEP = RWKV_SUB * RWKV_CHUNK
CONV_K = 4
EPS = 1e-6
RWKV_GN_EPS = 64e-5

D_SSD_IN = SSD_WIDTH + D_XBC + SSD_HEADS
D_MLSTM_IN = 2 * MLSTM_WIDTH + 2 * MLSTM_HEADS
D_RWKV_IN = 3 * RWKV_WIDTH + 64 + 64 + 128

U_SSD_W = SSD_WIDTH + D_XBC
U_ML_W = 2 * MLSTM_WIDTH
U_RW_W = D_RWKV_IN
U_SM_W = 128
SM_DT = 0
SM_I = SSD_HEADS
SM_F = SSD_HEADS + MLSTM_HEADS

HALO = 8
V7X_VMEM_LIMIT = 56 * 1024 * 1024

assert MIX_STEP == SSD_CHUNK == MLSTM_CHUNK


def _silu(x):
    return x * jax.nn.sigmoid(x)


def _softplus(x):
    return jnp.maximum(x, 0.0) + jnp.log1p(jnp.exp(-jnp.abs(x)))


def _log_sigmoid(x):
    return -_softplus(-x)


def _rms(x, g):
    return x * lax.rsqrt(jnp.mean(x * x, axis=-1, keepdims=True) + EPS) * g


def _dot(a, b):
    return jnp.dot(a, b, preferred_element_type=jnp.float32)


def _dot_nt(a, b):
    return lax.dot_general(a, b, (((1,), (1,)), ((), ())), preferred_element_type=jnp.float32)


def _dot_tn(a, b):
    return lax.dot_general(a, b, (((0,), (0,)), ((), ())), preferred_element_type=jnp.float32)


def _bf(x):
    return x.astype(jnp.bfloat16)


def _params(*sem):
    return pltpu.CompilerParams(dimension_semantics=sem, vmem_limit_bytes=V7X_VMEM_LIMIT)


FFN_TM = 512
FFN_FC = 1408


def _resident(shape):
    return pl.BlockSpec(shape, lambda i: (0,) * len(shape), pipeline_mode=pl.Buffered(1))


def _ffn_half_step(x, g_ref, wgu_ref, wd_ref):
    h = _bf(_rms(x, g_ref[...]))
    acc = x
    for lo in range(0, D_FF, FFN_FC):
        gate = _dot(h, wgu_ref[:, lo:lo + FFN_FC])
        up = _dot(h, wgu_ref[:, D_FF + lo:D_FF + lo + FFN_FC])
        acc = acc + 0.5 * _dot(_bf(_silu(gate) * up), wd_ref[lo:lo + FFN_FC, :])
    return acc


def _ffn_kernel(x_ref, g_ref, wgu_ref, wd_ref, o_ref):
    o_ref[...] = _ffn_half_step(x_ref[...], g_ref, wgu_ref, wd_ref)


def _ffn(x2, g, w_gate_up, w_down):
    n = x2.shape[0]
    tok = pl.BlockSpec((FFN_TM, D_MODEL), lambda i: (i, 0))
    return pl.pallas_call(
        _ffn_kernel,
        grid=(n // FFN_TM,),
        in_specs=[tok, _resident((1, D_MODEL)), _resident((D_MODEL, 2 * D_FF)), _resident((D_FF, D_MODEL))],
        out_specs=tok,
        out_shape=jax.ShapeDtypeStruct((n, D_MODEL), jnp.float32),
        compiler_params=_params("parallel"),
        name="ffn",
    )(x2, g, w_gate_up, w_down)


def _mix_ffn_kernel(x_ref, ys_ref, ym_ref, yr_ref, ws_ref, wm_ref, wr_ref, g_ref, wgu_ref, wd_ref, fg_ref,
                    o_ref, *, final_norm):
    x = (x_ref[...] + _dot(ys_ref[...], ws_ref[...]) + _dot(ym_ref[...], wm_ref[...])
         + _dot(yr_ref[...], wr_ref[...]))
    y = _ffn_half_step(x, g_ref, wgu_ref, wd_ref)
    if final_norm:
        y = _rms(y, fg_ref[...])
    o_ref[...] = y


def _mix_ffn(x2, y_ssd, y_ml, y_rw, w_s, w_m, w_r, g, w_gate_up, w_down, final_g, final_norm):
    n = x2.shape[0]
    tok = lambda w: pl.BlockSpec((FFN_TM, w), lambda i: (i, 0))
    return pl.pallas_call(
        functools.partial(_mix_ffn_kernel, final_norm=final_norm),
        grid=(n // FFN_TM,),
        in_specs=[tok(D_MODEL), tok(SSD_WIDTH), tok(MLSTM_WIDTH), tok(RWKV_WIDTH),
                  _resident((SSD_WIDTH, D_MODEL)), _resident((MLSTM_WIDTH, D_MODEL)),
                  _resident((RWKV_WIDTH, D_MODEL)), _resident((1, D_MODEL)),
                  _resident((D_MODEL, 2 * D_FF)), _resident((D_FF, D_MODEL)), _resident((1, D_MODEL))],
        out_specs=tok(D_MODEL),
        out_shape=jax.ShapeDtypeStruct((n, D_MODEL), jnp.float32),
        compiler_params=_params("parallel"),
        name="mix_ffn",
    )(x2, y_ssd, y_ml, y_rw, w_s, w_m, w_r, g, w_gate_up, w_down, final_g)


def _causal_conv(ext_ref, cur, w_ref, b_ref, first):
    rows = cur.shape[0]

    @pl.when(first)
    def _():
        ext_ref[0:HALO, :] = jnp.zeros((HALO, cur.shape[1]), jnp.float32)

    ext_ref[HALO:HALO + rows, :] = cur
    acc = b_ref[...] + cur * w_ref[CONV_K - 1:CONV_K, :]
    for j in range(CONV_K - 1):
        acc = acc + ext_ref[pl.ds(HALO - (CONV_K - 1) + j, rows), :] * w_ref[j:j + 1, :]
    ext_ref[0:HALO, :] = cur[rows - HALO:rows, :]
    return acc


def _split_terms(x, terms):
    parts, rest = [], x
    for i in range(terms):
        p = _bf(rest)
        parts.append(p)
        if i + 1 < terms:
            rest = rest - p.astype(jnp.float32)
    return parts


def _split_lanes(x, terms):
    return jnp.concatenate(_split_terms(x, terms), axis=1)


def _split_rows(x, terms):
    return jnp.concatenate(_split_terms(x, terms), axis=0)


def _ssd_steps(first, u_ref, sm_ref, cw_ref, cb_ref, dtb_ref, nega_ref, dskip_ref, ng_ref, tril_ref, expand_ref,
               selcol_ref, y_ref, ext_ref, st_ref):
    L, P, N, G = SSD_CHUNK, SSD_HEAD_DIM, SSD_STATE, SSD_GROUPS
    H = SSD_HEADS

    @pl.when(first)
    def _():
        st_ref[...] = jnp.zeros_like(st_ref)

    z = u_ref[:, 0:SSD_WIDTH]
    xbc = _silu(_causal_conv(ext_ref, u_ref[:, SSD_WIDTH:U_SSD_W], cw_ref, cb_ref, first))
    yield
    xs = xbc[:, 0:SSD_WIDTH]
    b_in = xbc[:, SSD_WIDTH:SSD_WIDTH + G * N]
    c_in = xbc[:, SSD_WIDTH + G * N:SSD_WIDTH + 2 * G * N]
    dt = _softplus(sm_ref[...] + dtb_ref[...])
    a = dt * nega_ref[...]
    a_cum = _dot(tril_ref[...], _split_rows(a, 3))
    a_cum_t = a_cum.T
    a_cum2 = _split_lanes(a_cum, 2)
    a_wide = _dot(a_cum2, expand_ref[...])
    col_b = _dot(a_cum2, selcol_ref[...])
    dt_wide = _dot(_split_lanes(dt, 2), expand_ref[...])
    exp_a = jnp.exp(a_wide)
    x_dt = xs * dt_wide
    xdd = x_dt * jnp.exp(a_wide[L - 1:L, :] - a_wide)

    yield
    row = lax.broadcasted_iota(jnp.int32, (L, L), 0)
    col = lax.broadcasted_iota(jnp.int32, (L, L), 1)
    causal = row >= col
    group0 = col < N
    c_b = _bf(c_in)
    b_b = _bf(b_in)
    cbs = [_dot_nt(_bf(jnp.where(group0 if g == 0 else ~group0, c_in, 0.0)), b_b) for g in range(G)]
    st = st_ref[...]
    y_off = _dot(c_b, _bf(st))

    yield
    x_dt_b = _bf(x_dt)
    y_parts = []
    for hp in range(H // 2):
        cb_g = cbs[(2 * hp) // (H // G)]
        x_pair = x_dt_b[:, hp * 2 * P:(hp + 1) * 2 * P]
        halves = []
        for h in (2 * hp, 2 * hp + 1):
            seg = col_b[:, h * 128:(h + 1) * 128] - a_cum_t[h:h + 1, :]
            halves.append(_dot(_bf(cb_g * jnp.exp(jnp.where(causal, seg, -jnp.inf))), x_pair))
        y_parts.append(jnp.where(group0, halves[0], halves[1]))
        yield
    y = jnp.concatenate(y_parts, axis=1) + y_off * exp_a + xs * dskip_ref[...]

    srow = lax.broadcasted_iota(jnp.int32, (G * N, H * P), 0)
    scol = lax.broadcasted_iota(jnp.int32, (G * N, H * P), 1)
    own = (srow >= N) == (scol >= (H // G) * P)
    st_ref[...] = st * exp_a[L - 1:L, :] + jnp.where(own, _dot_tn(b_b, _bf(xdd)), 0.0)

    yield
    y = y * _silu(z)
    gw = SSD_WIDTH // G
    y_ref[...] = _bf(jnp.concatenate(
        [_rms(y[:, g * gw:(g + 1) * gw], ng_ref[:, g * gw:(g + 1) * gw]) for g in range(G)], axis=1))


def _mlstm_steps(first, u_ref, sm_ref, cw_ref, cb_ref, wqk_ref, wv_ref, gb_ref, ng_ref, tril_ref, selcol_ref,
                 y_ref, ext_ref, cn_ref, m_ref):
    L, H, Dh = MLSTM_CHUNK, MLSTM_HEADS, MLSTM_HEAD_DIM

    @pl.when(first)
    def _():
        cn_ref[...] = jnp.zeros_like(cn_ref)
        m_ref[...] = jnp.zeros_like(m_ref)

    xm = u_ref[:, 0:MLSTM_WIDTH]
    o_pre = u_ref[:, MLSTM_WIDTH:U_ML_W]
    xc_b = _bf(_silu(_causal_conv(ext_ref, xm, cw_ref, cb_ref, first)))
    xm_b = _bf(xm)
    yield
    gates = sm_ref[...] + gb_ref[...]
    log_f = _log_sigmoid(gates)
    bcum = _dot(tril_ref[...], _split_rows(log_f, 3))
    bcum_t = bcum.T
    gates_t = gates.T
    col_f = _dot(_split_lanes(bcum, 2), selcol_ref[:, 0:H * 128])
    col_i = _dot(_split_lanes(gates, 2), selcol_ref[:, H * 128:2 * H * 128])
    row = lax.broadcasted_iota(jnp.int32, (L, L), 0)
    col = lax.broadcasted_iota(jnp.int32, (L, L), 1)
    causal = row >= col
    ones = jnp.ones((L, Dh), jnp.bfloat16)
    twice = lambda x: jnp.concatenate([x, x], axis=1)
    heads = range(H)
    sls = [slice(h * Dh, (h + 1) * Dh) for h in heads]

    yield
    qk_proj = [_dot(xc_b[:, sls[h]], wqk_ref[h]) for h in heads]
    q_b = [_bf(p[:, 0:Dh]) for p in qk_proj]
    k = [p[:, Dh:2 * Dh] * (Dh ** -0.5) for p in qk_proj]
    v1 = [jnp.concatenate([_bf(_dot(xm_b[:, sls[h]], wv_ref[h])), ones], axis=1) for h in heads]
    b_col = [col_f[:, h * 128:(h + 1) * 128] for h in heads]
    i_col = [col_i[:, h * 128:(h + 1) * 128] for h in heads]
    b_end = [b[L - 1:L, :] for b in b_col]
    m_st = [m_ref[h:h + 1, :] for h in heads]
    cn_st = [cn_ref[h] for h in heads]

    yield
    d_log = [jnp.where(causal, b_col[h] - bcum_t[SM_F + h:SM_F + h + 1, :] + gates_t[SM_I + h:SM_I + h + 1, :],
                       -jnp.inf) for h in heads]
    inter_log = [b_col[h] + m_st[h] for h in heads]
    m_t = [jnp.maximum(inter_log[h], jnp.max(d_log[h], axis=1, keepdims=True)) for h in heads]
    qk = [_bf(_dot_nt(q_b[h], _bf(k[h])) * jnp.exp(d_log[h] - m_t[h])) for h in heads]
    inter = [_dot(q_b[h], _bf(cn_st[h])) for h in heads]
    yield
    outs = []
    for h in heads:
        nd = _dot(qk[h], v1[h]) + inter[h] * twice(jnp.exp(inter_log[h] - m_t[h]))
        hh = nd[:, 0:Dh] / jnp.maximum(jnp.abs(nd[:, Dh:2 * Dh]), jnp.exp(-m_t[h]))
        hh = hh * jax.nn.sigmoid(o_pre[:, sls[h]])
        outs.append(_rms(hh, ng_ref[:, sls[h]]))
    y_ref[...] = _bf(jnp.concatenate(outs, axis=1))
    yield

    for h in heads:
        end_log = b_end[h] - b_col[h] + i_col[h]
        m_loc = jnp.max(end_log, axis=0, keepdims=True)
        kw = k[h] * jnp.exp(end_log - m_loc)
        cn_loc = _dot_tn(_bf(kw), v1[h])
        m_new = jnp.maximum(b_end[h] + m_st[h], m_loc)
        s_old = jnp.exp(b_end[h] + m_st[h] - m_new)
        s_new = jnp.exp(m_loc - m_new)
        cn_ref[h] = cn_st[h] * twice(s_old) + cn_loc * twice(s_new)
        m_ref[h:h + 1, :] = m_new


def _unit_lower_inverse(a_list, idx_r, idx_c, n):
    size = a_list[0].shape[0]
    eye = (idx_r == idx_c).astype(jnp.float32)
    base = (idx_r >> 1) == (idx_c >> 1)
    ts = [eye + jnp.where(base, a, 0.0) for a in a_list]
    log_b = 1
    while (1 << log_b) < n:
        b = 1 << log_b
        join = ((idx_r >> (log_b + 1)) == (idx_c >> (log_b + 1))) & ((idx_r >> log_b) != (idx_c >> log_b))
        es = [_bf(jnp.where(join, a, 0.0)) for a in a_list]
        tbs = [_bf(t) for t in ts]
        if b < 8:
            tes = [_bf(_dot(tb, e)) for tb, e in zip(tbs, es)]
            ts = [t + _dot(te, tb) for t, te, tb in zip(ts, tes, tbs)]
        else:
            lower = [slice(s + b, s + 2 * b) for s in range(0, size, 2 * b)]
            t_low = [jnp.concatenate([t[sl] for sl in lower], axis=0) for t in ts]
            tes = [_bf(_dot(_bf(tl), e)) for tl, e in zip(t_low, es)]
            upd = [tl + _dot(te, tb) for tl, te, tb in zip(t_low, tes, tbs)]
            ts = [jnp.concatenate([piece for j, sl in enumerate(lower)
                                   for piece in (t[sl.start - b:sl.start], u[j * b:(j + 1) * b])], axis=0)
                  for t, u in zip(ts, upd)]
        log_b += 1
        yield
    return ts


def _head_sum(x, hsum_bf):
    rows = x.shape[0]
    s = _dot(_split_rows(x, 2), hsum_bf)
    return s[0:rows] + s[rows:2 * rows]


def _rwkv_steps(first, u_ref, mu_ref, w0_ref, wup_ref, a0_ref, aup_ref, gup_ref, kk_ref, ka_ref, rk_ref,
                lnw_ref, lnb_ref, tril_ref, hsum_ref, y_ref, ext_ref, s_ref):
    C, H, N, W, SUB = RWKV_CHUNK, RWKV_HEADS, RWKV_HEAD_DIM, RWKV_WIDTH, RWKV_SUB
    nb = u_ref.shape[0]
    rows = SUB * C
    nseg = nb * SUB

    @pl.when(first)
    def _():
        ext_ref[:, 0:HALO, :] = jnp.zeros((nb, HALO, U_RW_W), jnp.float32)
        s_ref[...] = jnp.zeros_like(s_ref)

    u_rows, prev_rows = [], []
    for b in range(nb):
        ub = u_ref[b]
        ext_ref[b, HALO:HALO + rows, :] = ub
        prev_rows.append(ext_ref[b, pl.ds(HALO - 1, rows), :])
        ext_ref[b, 0:HALO, :] = ub[rows - HALO:rows, :]
        u_rows.append(ub)
    u = jnp.concatenate(u_rows, axis=0)
    u_prev = jnp.concatenate(prev_rows, axis=0)
    yield
    us = u + (u_prev - u) * mu_ref[...]
    r = us[:, 0:W]
    k = us[:, W:2 * W]
    v = us[:, 2 * W:3 * W]
    wa_l = us[:, 3 * W:3 * W + 128]
    g_l = us[:, 3 * W + 128:3 * W + 256]
    w_log = -_softplus(-(w0_ref[...] + _dot(_bf(jnp.tanh(wa_l)), wup_ref[...]))) - 0.5
    ld = -jnp.exp(w_log)
    a = jax.nn.sigmoid(a0_ref[...] + _dot(_bf(wa_l), aup_ref[...]))
    g = _dot(_bf(jax.nn.sigmoid(g_l)), gup_ref[...])
    yield
    hsum = hsum_ref[...]
    kk = k * kk_ref[...]
    kk = kk / jnp.maximum(jnp.sqrt(_head_sum(kk * kk, hsum)), 1e-6)
    k = k * (1.0 + (a - 1.0) * ka_ref[...])

    yield
    cl = _dot(tril_ref[...], _split_rows(ld, 3))
    cl_end = jnp.concatenate([jnp.broadcast_to(cl[(s + 1) * C - 1:(s + 1) * C, :], (C, W)) for s in range(nseg)],
                             axis=0)
    p_inv = jnp.exp(-cl)
    p_end = jnp.exp(cl_end - cl)
    a_t = -kk * jnp.exp(cl - ld)
    kka = kk * a
    b_t = kka * p_inv
    k_t = k * p_inv
    r_t = r * jnp.exp(cl)
    b_e = kka * p_end
    k_e = k * p_end

    yield
    PW = 2 * N
    idx_r = lax.broadcasted_iota(jnp.int32, (PW, PW), 0)
    idx_c = lax.broadcasted_iota(jnp.int32, (PW, PW), 1)
    same_head = (idx_r >> 6) == (idx_c >> 6)
    strict = same_head & (idx_r > idx_c)
    incl = same_head & (idx_r >= idx_c)
    lane = lax.broadcasted_iota(jnp.int32, (C, PW), 1)
    head0 = lane < N

    def stack(x):
        return _bf(jnp.concatenate([jnp.where(head0, x, 0.0), jnp.where(head0, 0.0, x)], axis=0))

    npair = H // 2
    probs = [(s, p) for s in range(nseg) for p in range(npair)]
    cut = lambda x, s, p: x[s * C:(s + 1) * C, p * PW:(p + 1) * PW]
    at_s = [stack(cut(a_t, s, p)) for s, p in probs]
    rt_s = [stack(cut(r_t, s, p)) for s, p in probs]
    v_s = [stack(cut(v, s, p)) for s, p in probs]
    yield
    aas = []
    for q, (s, p) in enumerate(probs):
        bt, kt = _bf(cut(b_t, s, p)), _bf(cut(k_t, s, p))
        aas.append(_dot_nt(jnp.concatenate([at_s[q], rt_s[q]], axis=0), jnp.concatenate([bt, bt, kt, kt], axis=0)))
    yield
    t_inv = yield from _unit_lower_inverse([jnp.where(strict, aa[0:PW, 0:PW], 0.0) for aa in aas], idx_r, idx_c, C)
    t_inv = [_bf(t) for t in t_inv]

    sts = [s_ref[i] for i in range(nb * npair)]
    y_parts = [None] * len(probs)
    for j in range(SUB):
        qs = [((b * SUB + j) * npair + p, b * npair + p) for b in range(nb) for p in range(npair)]
        rhs = [_dot(jnp.concatenate([at_s[q], _bf(jnp.where(strict, aas[q][0:PW, PW:2 * PW], 0.0))], axis=1),
                    jnp.concatenate([_bf(sts[i]), v_s[q]], axis=0)) for q, i in qs]
        yield
        uu = [_bf(_dot(t_inv[q], _bf(r_))) for (q, i), r_ in zip(qs, rhs)]
        yield
        for (q, i), u_ in zip(qs, uu):
            a_rb = jnp.where(incl, aas[q][PW:2 * PW, 0:PW], 0.0)
            a_rk = jnp.where(incl, aas[q][PW:2 * PW, PW:2 * PW], 0.0)
            yy = _dot(jnp.concatenate([rt_s[q], _bf(a_rb), _bf(a_rk)], axis=1),
                      jnp.concatenate([_bf(sts[i]), u_, v_s[q]], axis=0))
            y_parts[q] = yy[0:C] + yy[C:2 * C]
        yield
        for (q, i), u_ in zip(qs, uu):
            s, p = probs[q]
            ld_end_col = jnp.broadcast_to(cl[(s + 1) * C - 1:(s + 1) * C, p * PW:(p + 1) * PW], (PW, PW)).T
            sts[i] = sts[i] * jnp.exp(ld_end_col) + _dot_tn(
                jnp.concatenate([stack(cut(b_e, s, p)), stack(cut(k_e, s, p))], axis=0),
                jnp.concatenate([u_, v_s[q]], axis=0))
        yield
    for i, st in enumerate(sts):
        s_ref[i] = st

    y = jnp.concatenate([jnp.concatenate(y_parts[s * npair:(s + 1) * npair], axis=1) for s in range(nseg)], axis=0)
    mu = _head_sum(y, hsum) * (1.0 / N)
    yield
    yc = y - mu
    var = _head_sum(yc * yc, hsum) * (1.0 / N)
    yield
    y = yc * lax.rsqrt(var + RWKV_GN_EPS) * lnw_ref[...] + lnb_ref[...]
    y = y + _head_sum(r * k * rk_ref[...], hsum) * v
    y = y * g
    for b in range(nb):
        y_ref[b] = _bf(y[b * rows:(b + 1) * rows])


def _chain(generators):
    for gen in generators:
        yield from gen


def _interleave(streams):
    live = list(streams)
    while live:
        for item in list(live):
            gen, per_turn = item
            try:
                for _ in range(per_turn):
                    next(gen)
            except StopIteration:
                live.remove(item)


PROJ_COLS = 768


def _project_steps(x_ref, g_ref, w_refs, u_refs):
    nb, rows, _ = x_ref.shape
    h = _bf(_rms(x_ref[...].reshape(nb * rows, D_MODEL), g_ref[...]))
    yield
    for w_ref, u_ref in zip(w_refs, u_refs):
        width = w_ref.shape[1]
        for lo in range(0, width, PROJ_COLS):
            hi = min(lo + PROJ_COLS, width)
            part = _dot(h, w_ref[:, lo:hi])
            for b in range(nb):
                u_ref[b, :, lo:hi] = part[b * rows:(b + 1) * rows]
            yield


def _mixers_kernel(x_ref, xn_ref, p_g, p_wssd, p_wml, p_wrw, p_wsm,
                   s_cw, s_cb, s_dtb, s_nega, s_dskip, s_ng, tril_ref, s_expand, s_selcol,
                   m_cw, m_cb, m_wqk, m_wv, m_gb, m_ng, m_selcol,
                   r_mu, r_w0, r_wup, r_a0, r_aup, r_gup, r_kk, r_ka, r_rk, r_lnw, r_lnb, r_tril, r_hsum,
                   ys_ref, ym_ref, yr_ref,
                   us_buf, um_buf, ur_buf, sm_buf, s_ext, s_st, m_ext, m_cn, m_m, r_ext, r_s):
    step = pl.program_id(0)
    first = step == 0
    cur = step % 2
    nxt = 1 - cur
    nb = x_ref.shape[0]
    p_w = (p_wssd, p_wml, p_wrw, p_wsm)
    bufs = (us_buf, um_buf, ur_buf, sm_buf)

    @pl.when(first)
    def _():
        for _ in _project_steps(x_ref, p_g, p_w, [buf.at[0] for buf in bufs]):
            pass

    us_ref, um_ref, ur_ref, sm_ref = (buf.at[cur] for buf in bufs)
    project = _project_steps(xn_ref, p_g, p_w, [buf.at[nxt] for buf in bufs])
    rwkv = _rwkv_steps(first, ur_ref, r_mu, r_w0, r_wup, r_a0, r_aup, r_gup, r_kk, r_ka, r_rk, r_lnw, r_lnb,
                       r_tril, r_hsum, yr_ref, r_ext, r_s)
    vec = []
    for b in range(nb):
        vec.append(_ssd_steps(first, us_ref.at[b], sm_ref.at[b], s_cw, s_cb, s_dtb, s_nega, s_dskip, s_ng,
                              tril_ref, s_expand, s_selcol, ys_ref.at[b], s_ext.at[b], s_st.at[b]))
    for b in range(nb):
        vec.append(_mlstm_steps(first, um_ref.at[b], sm_ref.at[b], m_cw, m_cb, m_wqk, m_wv, m_gb, m_ng,
                                tril_ref, m_selcol, ym_ref.at[b], m_ext.at[b], m_cn.at[b], m_m.at[b]))
    _interleave([(rwkv, 1), (_chain(vec), 2), (project, 1)])


def _mixers(x3, proj_consts, ssd_consts, ml_consts, rw_consts, tril3):
    bsz, t, _ = x3.shape
    nblk = t // MIX_STEP
    tok = lambda w: pl.BlockSpec((bsz, MIX_STEP, w), lambda c: (0, c, 0))
    tok_next = pl.BlockSpec((bsz, MIX_STEP, D_MODEL), lambda c: (0, jnp.minimum(c + 1, nblk - 1), 0))
    whole = lambda arr: _resident(arr.shape)
    consts = (list(proj_consts) + list(ssd_consts[:6]) + [tril3] + list(ssd_consts[6:]) + list(ml_consts)
              + list(rw_consts))
    H, Dh = MLSTM_HEADS, MLSTM_HEAD_DIM
    u_buf = lambda w: pltpu.VMEM((2, bsz, MIX_STEP, w), jnp.float32)
    return pl.pallas_call(
        _mixers_kernel,
        grid=(nblk,),
        in_specs=[tok(D_MODEL), tok_next] + [whole(c) for c in consts],
        out_specs=[tok(SSD_WIDTH), tok(MLSTM_WIDTH), tok(RWKV_WIDTH)],
        out_shape=[jax.ShapeDtypeStruct((bsz, t, w), jnp.bfloat16) for w in (SSD_WIDTH, MLSTM_WIDTH, RWKV_WIDTH)],
        scratch_shapes=[
            u_buf(U_SSD_W), u_buf(U_ML_W), u_buf(U_RW_W), u_buf(U_SM_W),
            pltpu.VMEM((bsz, HALO + MIX_STEP, D_XBC), jnp.float32),
            pltpu.VMEM((bsz, SSD_GROUPS * SSD_STATE, SSD_WIDTH), jnp.float32),
            pltpu.VMEM((bsz, HALO + MIX_STEP, MLSTM_WIDTH), jnp.float32),
            pltpu.VMEM((bsz, H, Dh, 2 * Dh), jnp.float32),
            pltpu.VMEM((bsz, 8, 128), jnp.float32),
            pltpu.VMEM((bsz, HALO + MIX_STEP, U_RW_W), jnp.float32),
            pltpu.VMEM((bsz * RWKV_HEADS // 2, 2 * RWKV_HEAD_DIM, 2 * RWKV_HEAD_DIM), jnp.float32),
        ],
        compiler_params=_params("arbitrary"),
        name="mixers",
    )(x3, x3, *consts)


def _pad_lanes(vec, offset, width=U_SM_W):
    out = jnp.zeros((1, width), jnp.float32)
    return out.at[0, offset:offset + vec.shape[0]].set(vec)


def kernel(x, ffn1_norm, ffn1_w_gate_up, ffn1_w_down, mix_norm, w_in, ssd_conv_w, ssd_conv_b, ssd_dt_bias, ssd_a_log, ssd_d, ssd_norm, mlstm_conv_w, mlstm_conv_b, mlstm_wq, mlstm_wk, mlstm_wv, mlstm_gate_bias, mlstm_norm, rwkv_shift_mu, rwkv_w0, rwkv_w_up, rwkv_a0, rwkv_a_up, rwkv_g_up, rwkv_k_k, rwkv_k_a, rwkv_r_k, rwkv_ln_w, rwkv_ln_b, w_out, ffn2_norm, ffn2_w_gate_up, ffn2_w_down, final_norm):
    bsz, t, d = x.shape
    depth = w_in.shape[0]
    n = bsz * t
    f32 = jnp.float32
    row = lambda v: v.reshape(1, -1).astype(f32)

    tril_l = jnp.tril(jnp.ones((SSD_CHUNK, SSD_CHUNK), f32))
    tril3 = _bf(jnp.concatenate([tril_l] * 3, axis=1))
    lane_head = jnp.arange(SSD_WIDTH) // SSD_HEAD_DIM
    expand1 = (jnp.arange(U_SM_W)[:, None] == lane_head[None, :]).astype(f32)
    expand = _bf(jnp.concatenate([expand1] * 2, axis=0))
    tile_head = jnp.arange(SSD_HEADS * 128) // 128
    selcol1 = (jnp.arange(U_SM_W)[:, None] == tile_head[None, :]).astype(f32)
    selcol = _bf(jnp.concatenate([selcol1] * 2, axis=0))
    ml_src = jnp.concatenate([SM_F + jnp.arange(MLSTM_HEADS), SM_I + jnp.arange(MLSTM_HEADS)])
    ml_sel1 = (jnp.arange(U_SM_W)[:, None] == jnp.repeat(ml_src, 128)[None, :]).astype(f32)
    ml_selcol = _bf(jnp.concatenate([ml_sel1] * 2, axis=0))
    tril_c = jnp.kron(jnp.eye(bsz * RWKV_SUB, dtype=f32), jnp.tril(jnp.ones((RWKV_CHUNK, RWKV_CHUNK), f32)))
    tril_c = _bf(jnp.concatenate([tril_c] * 3, axis=1))
    head_id = jnp.arange(RWKV_WIDTH) // RWKV_HEAD_DIM
    hsum = _bf(head_id[:, None] == head_id[None, :])
    final_g = row(final_norm)

    x2 = x.reshape(n, d)
    for l in range(depth):
        wl = w_in[l]
        o_ml = D_SSD_IN
        o_rw = D_SSD_IN + D_MLSTM_IN
        w_ssd = _bf(wl[:, 0:U_SSD_W])
        w_ml = _bf(wl[:, o_ml:o_ml + U_ML_W])
        w_rw = _bf(wl[:, o_rw:o_rw + U_RW_W])
        w_sm = _bf(jnp.concatenate(
            [wl[:, U_SSD_W:D_SSD_IN], wl[:, o_ml + U_ML_W:o_rw],
             jnp.zeros((d, U_SM_W - SSD_HEADS - 2 * MLSTM_HEADS), f32)], axis=1))

        x2 = _ffn(x2, row(ffn1_norm[l]), _bf(ffn1_w_gate_up[l]), _bf(ffn1_w_down[l]))

        zeros64 = jnp.zeros((64, RWKV_WIDTH), f32)
        y_ssd, y_ml, y_rw = _mixers(
            x2.reshape(bsz, t, d), (row(mix_norm[l]), w_ssd, w_ml, w_rw, w_sm),
            (ssd_conv_w[l], row(ssd_conv_b[l]), _pad_lanes(ssd_dt_bias[l], SM_DT),
             _pad_lanes(-jnp.exp(ssd_a_log[l]), SM_DT), row(jnp.repeat(ssd_d[l], SSD_HEAD_DIM)), row(ssd_norm[l]),
             expand, selcol),
            (mlstm_conv_w[l], row(mlstm_conv_b[l]), _bf(jnp.concatenate([mlstm_wq[l], mlstm_wk[l]], axis=-1)),
             _bf(mlstm_wv[l]), _pad_lanes(mlstm_gate_bias[l], SM_I), row(mlstm_norm[l]), ml_selcol),
            (row(rwkv_shift_mu[l]), row(rwkv_w0[l]), _bf(jnp.concatenate([rwkv_w_up[l], zeros64], axis=0)),
             row(rwkv_a0[l]), _bf(jnp.concatenate([zeros64, rwkv_a_up[l]], axis=0)), _bf(rwkv_g_up[l]),
             row(rwkv_k_k[l]), row(rwkv_k_a[l]), row(rwkv_r_k[l]), row(rwkv_ln_w[l]), row(rwkv_ln_b[l]),
             tril_c, hsum),
            tril3)

        wo = w_out[l]
        x2 = _mix_ffn(x2, y_ssd.reshape(n, SSD_WIDTH), y_ml.reshape(n, MLSTM_WIDTH), y_rw.reshape(n, RWKV_WIDTH),
                      _bf(wo[0:SSD_WIDTH]), _bf(wo[SSD_WIDTH:SSD_WIDTH + MLSTM_WIDTH]),
                      _bf(wo[SSD_WIDTH + MLSTM_WIDTH:]),
                      row(ffn2_norm[l]), _bf(ffn2_w_gate_up[l]), _bf(ffn2_w_down[l]), final_g, l == depth - 1)
    return x2.reshape(bsz, t, d)
```

```python
import functools

import jax
import jax.numpy as jnp
from jax import lax
from jax.experimental import pallas as pl
from jax.experimental.pallas import tpu as pltpu

D_MODEL = 1024
D_FF = 2816
SSD_WIDTH = 1024
SSD_HEAD_DIM = 64
SSD_HEADS = 16
SSD_GROUPS = 2
SSD_STATE = 64
SSD_CHUNK = 128
D_XBC = SSD_WIDTH + 2 * SSD_GROUPS * SSD_STATE
MLSTM_WIDTH = 512
MLSTM_HEADS = 4
MLSTM_HEAD_DIM = 128
MLSTM_CHUNK = 128
RWKV_WIDTH = 512
RWKV_HEADS = 8
RWKV_HEAD_DIM = 64
RWKV_CHUNK = 64
RWKV_SUB = 2
MIX_STEP = RWKV_SUB * RWKV_CHUNK
CONV_K = 4
EPS = 1e-6
RWKV_GN_EPS = 64e-5

D_SSD_IN = SSD_WIDTH + D_XBC + SSD_HEADS
D_MLSTM_IN = 2 * MLSTM_WIDTH + 2 * MLSTM_HEADS
D_RWKV_IN = 3 * RWKV_WIDTH + 64 + 64 + 128

U_SSD_W = SSD_WIDTH + D_XBC
U_ML_W = 2 * MLSTM_WIDTH
U_RW_W = D_RWKV_IN
U_SM_W = 128
SM_DT = 0
SM_I = SSD_HEADS
SM_F = SSD_HEADS + MLSTM_HEADS

HALO = 8
V7X_VMEM_LIMIT = 56 * 1024 * 1024

assert MIX_STEP == SSD_CHUNK == MLSTM_CHUNK


def _silu(x):
    return x * jax.nn.sigmoid(x)


def _softplus(x):
    return jnp.maximum(x, 0.0) + jnp.log1p(jnp.exp(-jnp.abs(x)))


def _log_sigmoid(x):
    return -_softplus(-x)


def _rms(x, g):
    return x * lax.rsqrt(jnp.mean(x * x, axis=-1, keepdims=True) + EPS) * g


def _dot(a, b):
    return jnp.dot(a, b, preferred_element_type=jnp.float32)


def _dot_nt(a, b):
    return lax.dot_general(a, b, (((1,), (1,)), ((), ())), preferred_element_type=jnp.float32)


def _dot_tn(a, b):
    return lax.dot_general(a, b, (((0,), (0,)), ((), ())), preferred_element_type=jnp.float32)


def _bf(x):
    return x.astype(jnp.bfloat16)


def _params(*sem):
    return pltpu.CompilerParams(dimension_semantics=sem, vmem_limit_bytes=V7X_VMEM_LIMIT)


FFN_TM = 512
FFN_COLS = (1024, 1024, 768)
assert sum(FFN_COLS) == D_FF


def _resident(shape):
    return pl.BlockSpec(shape, lambda i: (0,) * len(shape), pipeline_mode=pl.Buffered(1))


def _ffn_half_step(x, g_ref, wgu_ref, wd_ref):
    h = _bf(_rms(x, g_ref[...]))
    acc = x
    lo = 0
    for width in FFN_COLS:
        gate = _dot(h, wgu_ref[:, lo:lo + width])
        up = _dot(h, wgu_ref[:, D_FF + lo:D_FF + lo + width])
        acc = acc + 0.5 * _dot(_bf(_silu(gate) * up), wd_ref[lo:lo + width, :])
        lo += width
    return acc


def _ffn_kernel(x_ref, g_ref, wgu_ref, wd_ref, o_ref):
    o_ref[...] = _ffn_half_step(x_ref[...], g_ref, wgu_ref, wd_ref)


def _ffn(x2, g, w_gate_up, w_down):
    n = x2.shape[0]
    tok = pl.BlockSpec((FFN_TM, D_MODEL), lambda i: (i, 0))
    return pl.pallas_call(
        _ffn_kernel,
        grid=(n // FFN_TM,),
        in_specs=[tok, _resident((1, D_MODEL)), _resident((D_MODEL, 2 * D_FF)), _resident((D_FF, D_MODEL))],
        out_specs=tok,
        out_shape=jax.ShapeDtypeStruct((n, D_MODEL), jnp.float32),
        compiler_params=_params("parallel"),
        name="ffn",
    )(x2, g, w_gate_up, w_down)


def _mix_ffn_kernel(x_ref, ys_ref, ym_ref, yr_ref, ws_ref, wm_ref, wr_ref, g_ref, wgu_ref, wd_ref, fg_ref,
                    o_ref, *, final_norm):
    x = (x_ref[...] + _dot(ys_ref[...], ws_ref[...]) + _dot(ym_ref[...], wm_ref[...])
         + _dot(yr_ref[...], wr_ref[...]))
    y = _ffn_half_step(x, g_ref, wgu_ref, wd_ref)
    if final_norm:
        y = _rms(y, fg_ref[...])
    o_ref[...] = y


def _mix_ffn(x2, y_ssd, y_ml, y_rw, w_s, w_m, w_r, g, w_gate_up, w_down, final_g, final_norm):
    n = x2.shape[0]
    tok = lambda w: pl.BlockSpec((FFN_TM, w), lambda i: (i, 0))
    return pl.pallas_call(
        functools.partial(_mix_ffn_kernel, final_norm=final_norm),
        grid=(n // FFN_TM,),
        in_specs=[tok(D_MODEL), tok(SSD_WIDTH), tok(MLSTM_WIDTH), tok(RWKV_WIDTH),
                  _resident((SSD_WIDTH, D_MODEL)), _resident((MLSTM_WIDTH, D_MODEL)),
                  _resident((RWKV_WIDTH, D_MODEL)), _resident((1, D_MODEL)),
                  _resident((D_MODEL, 2 * D_FF)), _resident((D_FF, D_MODEL)), _resident((1, D_MODEL))],
        out_specs=tok(D_MODEL),
        out_shape=jax.ShapeDtypeStruct((n, D_MODEL), jnp.float32),
        compiler_params=_params("parallel"),
        name="mix_ffn",
    )(x2, y_ssd, y_ml, y_rw, w_s, w_m, w_r, g, w_gate_up, w_down, final_g)


def _causal_conv(ext_ref, cur, w_ref, b_ref, first):
    rows = cur.shape[0]

    @pl.when(first)
    def _():
        ext_ref[0:HALO, :] = jnp.zeros((HALO, cur.shape[1]), jnp.float32)

    ext_ref[HALO:HALO + rows, :] = cur
    acc = b_ref[...] + cur * w_ref[CONV_K - 1:CONV_K, :]
    for j in range(CONV_K - 1):
        acc = acc + ext_ref[pl.ds(HALO - (CONV_K - 1) + j, rows), :] * w_ref[j:j + 1, :]
    ext_ref[0:HALO, :] = cur[rows - HALO:rows, :]
    return acc


def _split_terms(x, terms):
    parts, rest = [], x
    for i in range(terms):
        p = _bf(rest)
        parts.append(p)
        if i + 1 < terms:
            rest = rest - p.astype(jnp.float32)
    return parts


def _split_lanes(x, terms):
    return jnp.concatenate(_split_terms(x, terms), axis=1)


def _split_rows(x, terms):
    return jnp.concatenate(_split_terms(x, terms), axis=0)


def _ssd_steps(first, u_ref, sm_ref, cw_ref, cb_ref, dtb_ref, nega_ref, dskip_ref, ng_ref, tril_ref, expand_ref,
               selcol_ref, y_ref, ext_ref, st_ref):
    L, P, N, G = SSD_CHUNK, SSD_HEAD_DIM, SSD_STATE, SSD_GROUPS
    H = SSD_HEADS

    @pl.when(first)
    def _():
        st_ref[...] = jnp.zeros_like(st_ref)

    z = u_ref[:, 0:SSD_WIDTH]
    xbc = _silu(_causal_conv(ext_ref, u_ref[:, SSD_WIDTH:U_SSD_W], cw_ref, cb_ref, first))
    yield
    xs = xbc[:, 0:SSD_WIDTH]
    b_in = xbc[:, SSD_WIDTH:SSD_WIDTH + G * N]
    c_in = xbc[:, SSD_WIDTH + G * N:SSD_WIDTH + 2 * G * N]
    dt = _softplus(sm_ref[...] + dtb_ref[...])
    a = dt * nega_ref[...]
    a_cum = _dot(tril_ref[...], _split_rows(a, 3))
    a_cum_t = a_cum.T
    a_cum2 = _split_lanes(a_cum, 2)
    a_wide = _dot(a_cum2, expand_ref[...])
    col_b = _dot(a_cum2, selcol_ref[...])
    dt_wide = _dot(_split_lanes(dt, 2), expand_ref[...])
    exp_a = jnp.exp(a_wide)
    x_dt = xs * dt_wide
    xdd = x_dt * jnp.exp(a_wide[L - 1:L, :] - a_wide)

    yield
    row = lax.broadcasted_iota(jnp.int32, (L, L), 0)
    col = lax.broadcasted_iota(jnp.int32, (L, L), 1)
    causal = row >= col
    group0 = col < N
    c_b = _bf(c_in)
    b_b = _bf(b_in)
    cbs = [_dot_nt(_bf(jnp.where(group0 if g == 0 else ~group0, c_in, 0.0)), b_b) for g in range(G)]
    st = st_ref[...]
    y_off = _dot(c_b, _bf(st))

    yield
    x_dt_b = _bf(x_dt)
    y_parts = []
    for hp in range(H // 2):
        cb_g = cbs[(2 * hp) // (H // G)]
        x_pair = x_dt_b[:, hp * 2 * P:(hp + 1) * 2 * P]
        halves = []
        for h in (2 * hp, 2 * hp + 1):
            seg = col_b[:, h * 128:(h + 1) * 128] - a_cum_t[h:h + 1, :]
            halves.append(_dot(_bf(cb_g * jnp.exp(jnp.where(causal, seg, -jnp.inf))), x_pair))
        y_parts.append(jnp.where(group0, halves[0], halves[1]))
        yield
    y = jnp.concatenate(y_parts, axis=1) + y_off * exp_a + xs * dskip_ref[...]

    srow = lax.broadcasted_iota(jnp.int32, (G * N, H * P), 0)
    scol = lax.broadcasted_iota(jnp.int32, (G * N, H * P), 1)
    own = (srow >= N) == (scol >= (H // G) * P)
    st_ref[...] = st * exp_a[L - 1:L, :] + jnp.where(own, _dot_tn(b_b, _bf(xdd)), 0.0)

    yield
    y = y * _silu(z)
    gw = SSD_WIDTH // G
    y_ref[...] = _bf(jnp.concatenate(
        [_rms(y[:, g * gw:(g + 1) * gw], ng_ref[:, g * gw:(g + 1) * gw]) for g in range(G)], axis=1))


def _mlstm_steps(first, u_ref, sm_ref, cw_ref, cb_ref, wqk_ref, wv_ref, gb_ref, ng_ref, tril_ref, selcol_ref,
                 y_ref, ext_ref, cn_ref, m_ref):
    L, H, Dh = MLSTM_CHUNK, MLSTM_HEADS, MLSTM_HEAD_DIM

    @pl.when(first)
    def _():
        cn_ref[...] = jnp.zeros_like(cn_ref)
        m_ref[...] = jnp.zeros_like(m_ref)

    xm = u_ref[:, 0:MLSTM_WIDTH]
    o_pre = u_ref[:, MLSTM_WIDTH:U_ML_W]
    xc_b = _bf(_silu(_causal_conv(ext_ref, xm, cw_ref, cb_ref, first)))
    xm_b = _bf(xm)
    yield
    gates = sm_ref[...] + gb_ref[...]
    log_f = _log_sigmoid(gates)
    bcum = _dot(tril_ref[...], _split_rows(log_f, 3))
    bcum_t = bcum.T
    gates_t = gates.T
    col_f = _dot(_split_lanes(bcum, 2), selcol_ref[:, 0:H * 128])
    col_i = _dot(_split_lanes(gates, 2), selcol_ref[:, H * 128:2 * H * 128])
    row = lax.broadcasted_iota(jnp.int32, (L, L), 0)
    col = lax.broadcasted_iota(jnp.int32, (L, L), 1)
    causal = row >= col
    ones = jnp.ones((L, Dh), jnp.bfloat16)
    twice = lambda x: jnp.concatenate([x, x], axis=1)
    heads = range(H)
    sls = [slice(h * Dh, (h + 1) * Dh) for h in heads]

    yield
    qk_proj = [_dot(xc_b[:, sls[h]], wqk_ref[h]) for h in heads]
    q_b = [_bf(p[:, 0:Dh]) for p in qk_proj]
    k = [p[:, Dh:2 * Dh] * (Dh ** -0.5) for p in qk_proj]
    v1 = [jnp.concatenate([_bf(_dot(xm_b[:, sls[h]], wv_ref[h])), ones], axis=1) for h in heads]
    b_col = [col_f[:, h * 128:(h + 1) * 128] for h in heads]
    i_col = [col_i[:, h * 128:(h + 1) * 128] for h in heads]
    b_end = [b[L - 1:L, :] for b in b_col]
    m_st = [m_ref[h:h + 1, :] for h in heads]
    cn_st = [cn_ref[h] for h in heads]

    yield
    d_log = [jnp.where(causal, b_col[h] - bcum_t[SM_F + h:SM_F + h + 1, :] + gates_t[SM_I + h:SM_I + h + 1, :],
                       -jnp.inf) for h in heads]
    inter_log = [b_col[h] + m_st[h] for h in heads]
    m_t = [jnp.maximum(inter_log[h], jnp.max(d_log[h], axis=1, keepdims=True)) for h in heads]
    qk = [_bf(_dot_nt(q_b[h], _bf(k[h])) * jnp.exp(d_log[h] - m_t[h])) for h in heads]
    inter = [_dot(q_b[h], _bf(cn_st[h])) for h in heads]
    yield
    outs = []
    for h in heads:
        nd = _dot(qk[h], v1[h]) + inter[h] * twice(jnp.exp(inter_log[h] - m_t[h]))
        hh = nd[:, 0:Dh] / jnp.maximum(jnp.abs(nd[:, Dh:2 * Dh]), jnp.exp(-m_t[h]))
        hh = hh * jax.nn.sigmoid(o_pre[:, sls[h]])
        outs.append(_rms(hh, ng_ref[:, sls[h]]))
    y_ref[...] = _bf(jnp.concatenate(outs, axis=1))
    yield

    for h in heads:
        end_log = b_end[h] - b_col[h] + i_col[h]
        m_loc = jnp.max(end_log, axis=0, keepdims=True)
        kw = k[h] * jnp.exp(end_log - m_loc)
        cn_loc = _dot_tn(_bf(kw), v1[h])
        m_new = jnp.maximum(b_end[h] + m_st[h], m_loc)
        s_old = jnp.exp(b_end[h] + m_st[h] - m_new)
        s_new = jnp.exp(m_loc - m_new)
        cn_ref[h] = cn_st[h] * twice(s_old) + cn_loc * twice(s_new)
        m_ref[h:h + 1, :] = m_new


def _unit_lower_inverse(a_list, idx_r, idx_c, n):
    size = a_list[0].shape[0]
    eye = (idx_r == idx_c).astype(jnp.float32)
    base = (idx_r >> 1) == (idx_c >> 1)
    ts = [eye + jnp.where(base, a, 0.0) for a in a_list]
    log_b = 1
    while (1 << log_b) < n:
        b = 1 << log_b
        join = ((idx_r >> (log_b + 1)) == (idx_c >> (log_b + 1))) & ((idx_r >> log_b) != (idx_c >> log_b))
        es = [_bf(jnp.where(join, a, 0.0)) for a in a_list]
        tbs = [_bf(t) for t in ts]
        if b < 8:
            tes = [_bf(_dot(tb, e)) for tb, e in zip(tbs, es)]
            ts = [t + _dot(te, tb) for t, te, tb in zip(ts, tes, tbs)]
        else:
            lower = [slice(s + b, s + 2 * b) for s in range(0, size, 2 * b)]
            t_low = [jnp.concatenate([t[sl] for sl in lower], axis=0) for t in ts]
            tes = [_bf(_dot(_bf(tl), e)) for tl, e in zip(t_low, es)]
            upd = [tl + _dot(te, tb) for tl, te, tb in zip(t_low, tes, tbs)]
            ts = [jnp.concatenate([piece for j, sl in enumerate(lower)
                                   for piece in (t[sl.start - b:sl.start], u[j * b:(j + 1) * b])], axis=0)
                  for t, u in zip(ts, upd)]
        log_b += 1
        yield
    return ts


def _head_sum(x, hsum_bf):
    rows = x.shape[0]
    s = _dot(_split_rows(x, 2), hsum_bf)
    return s[0:rows] + s[rows:2 * rows]


def _rwkv_steps(first, u_ref, mu_ref, w0_ref, wup_ref, a0_ref, aup_ref, gup_ref, kk_ref, ka_ref, rk_ref,
                lnw_ref, lnb_ref, tril_ref, hsum_ref, y_ref, ext_ref, s_ref):
    C, H, N, W, SUB = RWKV_CHUNK, RWKV_HEADS, RWKV_HEAD_DIM, RWKV_WIDTH, RWKV_SUB
    nb = u_ref.shape[0]
    rows = SUB * C
    nseg = nb * SUB

    @pl.when(first)
    def _():
        ext_ref[:, 0:HALO, :] = jnp.zeros((nb, HALO, U_RW_W), jnp.float32)
        s_ref[...] = jnp.zeros_like(s_ref)

    u_rows, prev_rows = [], []
    for b in range(nb):
        ub = u_ref[b]
        ext_ref[b, HALO:HALO + rows, :] = ub
        prev_rows.append(ext_ref[b, pl.ds(HALO - 1, rows), :])
        ext_ref[b, 0:HALO, :] = ub[rows - HALO:rows, :]
        u_rows.append(ub)
    u = jnp.concatenate(u_rows, axis=0)
    u_prev = jnp.concatenate(prev_rows, axis=0)
    yield
    us = u + (u_prev - u) * mu_ref[...]
    r = us[:, 0:W]
    k = us[:, W:2 * W]
    v = us[:, 2 * W:3 * W]
    wa_l = us[:, 3 * W:3 * W + 128]
    g_l = us[:, 3 * W + 128:3 * W + 256]
    w_log = -_softplus(-(w0_ref[...] + _dot(_bf(jnp.tanh(wa_l)), wup_ref[...]))) - 0.5
    ld = -jnp.exp(w_log)
    a = jax.nn.sigmoid(a0_ref[...] + _dot(_bf(wa_l), aup_ref[...]))
    g = _dot(_bf(jax.nn.sigmoid(g_l)), gup_ref[...])
    yield
    hsum = hsum_ref[...]
    kk = k * kk_ref[...]
    kk = kk / jnp.maximum(jnp.sqrt(_head_sum(kk * kk, hsum)), 1e-6)
    k = k * (1.0 + (a - 1.0) * ka_ref[...])

    yield
    cl = _dot(tril_ref[...], _split_rows(ld, 3))
    cl_end = jnp.concatenate([jnp.broadcast_to(cl[(s + 1) * C - 1:(s + 1) * C, :], (C, W)) for s in range(nseg)],
                             axis=0)
    p_inv = jnp.exp(-cl)
    p_end = jnp.exp(cl_end - cl)
    a_t = -kk * jnp.exp(cl - ld)
    kka = kk * a
    b_t = kka * p_inv
    k_t = k * p_inv
    r_t = r * jnp.exp(cl)
    b_e = kka * p_end
    k_e = k * p_end

    yield
    PW = 2 * N
    idx_r = lax.broadcasted_iota(jnp.int32, (PW, PW), 0)
    idx_c = lax.broadcasted_iota(jnp.int32, (PW, PW), 1)
    same_head = (idx_r >> 6) == (idx_c >> 6)
    strict = same_head & (idx_r > idx_c)
    incl = same_head & (idx_r >= idx_c)
    lane = lax.broadcasted_iota(jnp.int32, (C, PW), 1)
    head0 = lane < N

    def stack(x):
        return _bf(jnp.concatenate([jnp.where(head0, x, 0.0), jnp.where(head0, 0.0, x)], axis=0))

    npair = H // 2
    probs = [(s, p) for s in range(nseg) for p in range(npair)]
    cut = lambda x, s, p: x[s * C:(s + 1) * C, p * PW:(p + 1) * PW]
    at_s = [stack(cut(a_t, s, p)) for s, p in probs]
    rt_s = [stack(cut(r_t, s, p)) for s, p in probs]
    v_s = [stack(cut(v, s, p)) for s, p in probs]
    yield
    a_ab, a_ak, a_rbk = [], [], []
    for q, (s, p) in enumerate(probs):
        bt, kt = _bf(cut(b_t, s, p)), _bf(cut(k_t, s, p))
        aa = _dot_nt(jnp.concatenate([at_s[q], rt_s[q]], axis=0), jnp.concatenate([bt, bt, kt, kt], axis=0))
        a_ab.append(jnp.where(strict, aa[0:PW, 0:PW], 0.0))
        a_ak.append(_bf(jnp.where(strict, aa[0:PW, PW:2 * PW], 0.0)))
        a_rbk.append(jnp.concatenate([_bf(jnp.where(incl, aa[PW:2 * PW, 0:PW], 0.0)),
                                      _bf(jnp.where(incl, aa[PW:2 * PW, PW:2 * PW], 0.0))], axis=1))
    yield
    t_inv = yield from _unit_lower_inverse(a_ab, idx_r, idx_c, C)
    t_inv = [_bf(t) for t in t_inv]

    sts = [s_ref[i] for i in range(nb * npair)]
    y_parts = [None] * len(probs)
    for j in range(SUB):
        qs = [((b * SUB + j) * npair + p, b * npair + p) for b in range(nb) for p in range(npair)]
        rhs = [_dot(jnp.concatenate([at_s[q], a_ak[q]], axis=1),
                    jnp.concatenate([_bf(sts[i]), v_s[q]], axis=0)) for q, i in qs]
        yield
        uu = [_bf(_dot(t_inv[q], _bf(r_))) for (q, i), r_ in zip(qs, rhs)]
        yield
        for (q, i), u_ in zip(qs, uu):
            yy = _dot(jnp.concatenate([rt_s[q], a_rbk[q]], axis=1),
                      jnp.concatenate([_bf(sts[i]), u_, v_s[q]], axis=0))
            y_parts[q] = yy[0:C] + yy[C:2 * C]
        yield
        for (q, i), u_ in zip(qs, uu):
            s, p = probs[q]
            ld_end_col = jnp.broadcast_to(cl[(s + 1) * C - 1:(s + 1) * C, p * PW:(p + 1) * PW], (PW, PW)).T
            sts[i] = sts[i] * jnp.exp(ld_end_col) + _dot_tn(
                jnp.concatenate([stack(cut(b_e, s, p)), stack(cut(k_e, s, p))], axis=0),
                jnp.concatenate([u_, v_s[q]], axis=0))
        yield
    for i, st in enumerate(sts):
        s_ref[i] = st

    y = jnp.concatenate([jnp.concatenate(y_parts[s * npair:(s + 1) * npair], axis=1) for s in range(nseg)], axis=0)
    mu = _head_sum(y, hsum) * (1.0 / N)
    yield
    yc = y - mu
    var = _head_sum(yc * yc, hsum) * (1.0 / N)
    yield
    y = yc * lax.rsqrt(var + RWKV_GN_EPS) * lnw_ref[...] + lnb_ref[...]
    y = y + _head_sum(r * k * rk_ref[...], hsum) * v
    y = y * g
    for b in range(nb):
        y_ref[b] = _bf(y[b * rows:(b + 1) * rows])


def _chain(generators):
    for gen in generators:
        yield from gen


def _interleave(streams):
    live = list(streams)
    while live:
        for item in list(live):
            gen, per_turn = item
            try:
                for _ in range(per_turn):
                    next(gen)
            except StopIteration:
                live.remove(item)


PROJ_COLS = 768


def _project_steps(x_ref, g_ref, w_refs, u_refs):
    nb, rows, _ = x_ref.shape
    h = _bf(_rms(x_ref[...].reshape(nb * rows, D_MODEL), g_ref[...]))
    yield
    for w_ref, u_ref in zip(w_refs, u_refs):
        width = w_ref.shape[1]
        for lo in range(0, width, PROJ_COLS):
            hi = min(lo + PROJ_COLS, width)
            part = _dot(h, w_ref[:, lo:hi])
            for b in range(nb):
                u_ref[b, :, lo:hi] = part[b * rows:(b + 1) * rows]
            yield


def _mixers_kernel(x_ref, xn_ref, p_g, p_wssd, p_wml, p_wrw, p_wsm,
                   s_cw, s_cb, s_dtb, s_nega, s_dskip, s_ng, tril_ref, s_expand, s_selcol,
                   m_cw, m_cb, m_wqk, m_wv, m_gb, m_ng, m_selcol,
                   r_mu, r_w0, r_wup, r_a0, r_aup, r_gup, r_kk, r_ka, r_rk, r_lnw, r_lnb, r_tril, r_hsum,
                   ys_ref, ym_ref, yr_ref,
                   us_buf, um_buf, ur_buf, sm_buf, s_ext, s_st, m_ext, m_cn, m_m, r_ext, r_s):
    step = pl.program_id(0)
    first = step == 0
    cur = step % 2
    nxt = 1 - cur
    nb = x_ref.shape[0]
    p_w = (p_wssd, p_wml, p_wrw, p_wsm)
    bufs = (us_buf, um_buf, ur_buf, sm_buf)

    @pl.when(first)
    def _():
        for _ in _project_steps(x_ref, p_g, p_w, [buf.at[0] for buf in bufs]):
            pass

    us_ref, um_ref, ur_ref, sm_ref = (buf.at[cur] for buf in bufs)
    project = _project_steps(xn_ref, p_g, p_w, [buf.at[nxt] for buf in bufs])
    rwkv = _rwkv_steps(first, ur_ref, r_mu, r_w0, r_wup, r_a0, r_aup, r_gup, r_kk, r_ka, r_rk, r_lnw, r_lnb,
                       r_tril, r_hsum, yr_ref, r_ext, r_s)
    vec = []
    for b in range(nb):
        vec.append(_ssd_steps(first, us_ref.at[b], sm_ref.at[b], s_cw, s_cb, s_dtb, s_nega, s_dskip, s_ng,
                              tril_ref, s_expand, s_selcol, ys_ref.at[b], s_ext.at[b], s_st.at[b]))
    for b in range(nb):
        vec.append(_mlstm_steps(first, um_ref.at[b], sm_ref.at[b], m_cw, m_cb, m_wqk, m_wv, m_gb, m_ng,
                                tril_ref, m_selcol, ym_ref.at[b], m_ext.at[b], m_cn.at[b], m_m.at[b]))
    _interleave([(rwkv, 1), (_chain(vec), 1), (project, 1)])


def _mixers(x3, proj_consts, ssd_consts, ml_consts, rw_consts, tril3):
    bsz, t, _ = x3.shape
    nblk = t // MIX_STEP
    tok = lambda w: pl.BlockSpec((bsz, MIX_STEP, w), lambda c: (0, c, 0))
    tok_next = pl.BlockSpec((bsz, MIX_STEP, D_MODEL), lambda c: (0, jnp.minimum(c + 1, nblk - 1), 0))
    whole = lambda arr: _resident(arr.shape)
    consts = (list(proj_consts) + list(ssd_consts[:6]) + [tril3] + list(ssd_consts[6:]) + list(ml_consts)
              + list(rw_consts))
    H, Dh = MLSTM_HEADS, MLSTM_HEAD_DIM
    u_buf = lambda w: pltpu.VMEM((2, bsz, MIX_STEP, w), jnp.float32)
    return pl.pallas_call(
        _mixers_kernel,
        grid=(nblk,),
        in_specs=[tok(D_MODEL), tok_next] + [whole(c) for c in consts],
        out_specs=[tok(SSD_WIDTH), tok(MLSTM_WIDTH), tok(RWKV_WIDTH)],
        out_shape=[jax.ShapeDtypeStruct((bsz, t, w), jnp.bfloat16) for w in (SSD_WIDTH, MLSTM_WIDTH, RWKV_WIDTH)],
        scratch_shapes=[
            u_buf(U_SSD_W), u_buf(U_ML_W), u_buf(U_RW_W), u_buf(U_SM_W),
            pltpu.VMEM((bsz, HALO + MIX_STEP, D_XBC), jnp.float32),
            pltpu.VMEM((bsz, SSD_GROUPS * SSD_STATE, SSD_WIDTH), jnp.float32),
            pltpu.VMEM((bsz, HALO + MIX_STEP, MLSTM_WIDTH), jnp.float32),
            pltpu.VMEM((bsz, H, Dh, 2 * Dh), jnp.float32),
            pltpu.VMEM((bsz, 8, 128), jnp.float32),
            pltpu.VMEM((bsz, HALO + MIX_STEP, U_RW_W), jnp.float32),
            pltpu.VMEM((bsz * RWKV_HEADS // 2, 2 * RWKV_HEAD_DIM, 2 * RWKV_HEAD_DIM), jnp.float32),
        ],
        compiler_params=_params("arbitrary"),
        name="mixers",
    )(x3, x3, *consts)


def _pad_lanes(vec, offset, width=U_SM_W):
    out = jnp.zeros((1, width), jnp.float32)
    return out.at[0, offset:offset + vec.shape[0]].set(vec)


def kernel(x, ffn1_norm, ffn1_w_gate_up, ffn1_w_down, mix_norm, w_in, ssd_conv_w, ssd_conv_b, ssd_dt_bias, ssd_a_log, ssd_d, ssd_norm, mlstm_conv_w, mlstm_conv_b, mlstm_wq, mlstm_wk, mlstm_wv, mlstm_gate_bias, mlstm_norm, rwkv_shift_mu, rwkv_w0, rwkv_w_up, rwkv_a0, rwkv_a_up, rwkv_g_up, rwkv_k_k, rwkv_k_a, rwkv_r_k, rwkv_ln_w, rwkv_ln_b, w_out, ffn2_norm, ffn2_w_gate_up, ffn2_w_down, final_norm):
    bsz, t, d = x.shape
    depth = w_in.shape[0]
    n = bsz * t
    f32 = jnp.float32
    row = lambda v: v.reshape(1, -1).astype(f32)

    tril_l = jnp.tril(jnp.ones((SSD_CHUNK, SSD_CHUNK), f32))
    tril3 = _bf(jnp.concatenate([tril_l] * 3, axis=1))
    lane_head = jnp.arange(SSD_WIDTH) // SSD_HEAD_DIM
    expand1 = (jnp.arange(U_SM_W)[:, None] == lane_head[None, :]).astype(f32)
    expand = _bf(jnp.concatenate([expand1] * 2, axis=0))
    tile_head = jnp.arange(SSD_HEADS * 128) // 128
    selcol1 = (jnp.arange(U_SM_W)[:, None] == tile_head[None, :]).astype(f32)
    selcol = _bf(jnp.concatenate([selcol1] * 2, axis=0))
    ml_src = jnp.concatenate([SM_F + jnp.arange(MLSTM_HEADS), SM_I + jnp.arange(MLSTM_HEADS)])
    ml_sel1 = (jnp.arange(U_SM_W)[:, None] == jnp.repeat(ml_src, 128)[None, :]).astype(f32)
    ml_selcol = _bf(jnp.concatenate([ml_sel1] * 2, axis=0))
    tril_c = jnp.kron(jnp.eye(bsz * RWKV_SUB, dtype=f32), jnp.tril(jnp.ones((RWKV_CHUNK, RWKV_CHUNK), f32)))
    tril_c = _bf(jnp.concatenate([tril_c] * 3, axis=1))
    head_id = jnp.arange(RWKV_WIDTH) // RWKV_HEAD_DIM
    hsum = _bf(head_id[:, None] == head_id[None, :])
    final_g = row(final_norm)

    x2 = x.reshape(n, d)
    for l in range(depth):
        wl = w_in[l]
        o_ml = D_SSD_IN
        o_rw = D_SSD_IN + D_MLSTM_IN
        w_ssd = _bf(wl[:, 0:U_SSD_W])
        w_ml = _bf(wl[:, o_ml:o_ml + U_ML_W])
        w_rw = _bf(wl[:, o_rw:o_rw + U_RW_W])
        w_sm = _bf(jnp.concatenate(
            [wl[:, U_SSD_W:D_SSD_IN], wl[:, o_ml + U_ML_W:o_rw],
             jnp.zeros((d, U_SM_W - SSD_HEADS - 2 * MLSTM_HEADS), f32)], axis=1))

        x2 = _ffn(x2, row(ffn1_norm[l]), _bf(ffn1_w_gate_up[l]), _bf(ffn1_w_down[l]))

        zeros64 = jnp.zeros((64, RWKV_WIDTH), f32)
        y_ssd, y_ml, y_rw = _mixers(
            x2.reshape(bsz, t, d), (row(mix_norm[l]), w_ssd, w_ml, w_rw, w_sm),
            (ssd_conv_w[l], row(ssd_conv_b[l]), _pad_lanes(ssd_dt_bias[l], SM_DT),
             _pad_lanes(-jnp.exp(ssd_a_log[l]), SM_DT), row(jnp.repeat(ssd_d[l], SSD_HEAD_DIM)), row(ssd_norm[l]),
             expand, selcol),
            (mlstm_conv_w[l], row(mlstm_conv_b[l]), _bf(jnp.concatenate([mlstm_wq[l], mlstm_wk[l]], axis=-1)),
             _bf(mlstm_wv[l]), _pad_lanes(mlstm_gate_bias[l], SM_I), row(mlstm_norm[l]), ml_selcol),
            (row(rwkv_shift_mu[l]), row(rwkv_w0[l]), _bf(jnp.concatenate([rwkv_w_up[l], zeros64], axis=0)),
             row(rwkv_a0[l]), _bf(jnp.concatenate([zeros64, rwkv_a_up[l]], axis=0)), _bf(rwkv_g_up[l]),
             row(rwkv_k_k[l]), row(rwkv_k_a[l]), row(rwkv_r_k[l]), row(rwkv_ln_w[l]), row(rwkv_ln_b[l]),
             tril_c, hsum),
            tril3)

        wo = w_out[l]
        x2 = _mix_ffn(x2, y_ssd.reshape(n, SSD_WIDTH), y_ml.reshape(n, MLSTM_WIDTH), y_rw.reshape(n, RWKV_WIDTH),
                      _bf(wo[0:SSD_WIDTH]), _bf(wo[SSD_WIDTH:SSD_WIDTH + MLSTM_WIDTH]),
                      _bf(wo[SSD_WIDTH + MLSTM_WIDTH:]),
                      row(ffn2_norm[l]), _bf(ffn2_w_gate_up[l]), _bf(ffn2_w_down[l]), final_g, l == depth - 1)
    return x2.reshape(bsz, t, d)
```

```python
import functools

import jax
import jax.numpy as jnp
from jax import lax
from jax.experimental import pallas as pl
from jax.experimental.pallas import tpu as pltpu

D_MODEL = 1024
D_FF = 2816
SSD_WIDTH = 1024
SSD_HEAD_DIM = 64
SSD_HEADS = 16
SSD_GROUPS = 2
SSD_STATE = 64
SSD_CHUNK = 128
D_XBC = SSD_WIDTH + 2 * SSD_GROUPS * SSD_STATE
MLSTM_WIDTH = 512
MLSTM_HEADS = 4
MLSTM_HEAD_DIM = 128
MLSTM_CHUNK = 128
RWKV_WIDTH = 512
RWKV_HEADS = 8
RWKV_HEAD_DIM = 64
RWKV_CHUNK = 64
RWKV_SUB = 2
MIX_STEP = RWKV_SUB * RWKV_CHUNK
CONV_K = 4
EPS = 1e-6
RWKV_GN_EPS = 64e-5

D_SSD_IN = SSD_WIDTH + D_XBC + SSD_HEADS
D_MLSTM_IN = 2 * MLSTM_WIDTH + 2 * MLSTM_HEADS
D_RWKV_IN = 3 * RWKV_WIDTH + 64 + 64 + 128

U_SSD_W = SSD_WIDTH + D_XBC
U_ML_W = 2 * MLSTM_WIDTH
U_RW_W = D_RWKV_IN
U_SM_W = 128
SM_DT = 0
SM_I = SSD_HEADS
SM_F = SSD_HEADS + MLSTM_HEADS

LANES = 128
SUBLANES = 8
HALO = SUBLANES
RWKV_LORA_W = 128
V7X_VMEM_LIMIT = 56 * 1024 * 1024

assert MIX_STEP == SSD_CHUNK == MLSTM_CHUNK


def _silu(x):
    return x * jax.nn.sigmoid(x)


def _softplus(x):
    return jnp.maximum(x, 0.0) + jnp.log1p(jnp.exp(-jnp.abs(x)))


def _log_sigmoid(x):
    return -_softplus(-x)


def _rms(x, g):
    return x * lax.rsqrt(jnp.mean(x * x, axis=-1, keepdims=True) + EPS) * g


def _dot(a, b):
    return jnp.dot(a, b, preferred_element_type=jnp.float32)


def _dot_nt(a, b):
    return lax.dot_general(a, b, (((1,), (1,)), ((), ())), preferred_element_type=jnp.float32)


def _dot_tn(a, b):
    return lax.dot_general(a, b, (((0,), (0,)), ((), ())), preferred_element_type=jnp.float32)


def _bf(x):
    return x.astype(jnp.bfloat16)


def _params(*sem):
    return pltpu.CompilerParams(dimension_semantics=sem, vmem_limit_bytes=V7X_VMEM_LIMIT)


FFN_TM = 512
FFN_COLS = (1024, 1024, 768)
assert sum(FFN_COLS) == D_FF


def _resident(shape):
    return pl.BlockSpec(shape, lambda i: (0,) * len(shape), pipeline_mode=pl.Buffered(1))


def _ffn_half_step(x, g_ref, wgu_ref, wd_ref):
    h = _bf(_rms(x, g_ref[...]))
    acc = x
    lo = 0
    for width in FFN_COLS:
        gate = _dot(h, wgu_ref[:, lo:lo + width])
        up = _dot(h, wgu_ref[:, D_FF + lo:D_FF + lo + width])
        acc = acc + 0.5 * _dot(_bf(_silu(gate) * up), wd_ref[lo:lo + width, :])
        lo += width
    return acc


def _ffn_kernel(x_ref, g_ref, wgu_ref, wd_ref, o_ref):
    o_ref[...] = _ffn_half_step(x_ref[...], g_ref, wgu_ref, wd_ref)


def _ffn(x2, g, w_gate_up, w_down):
    n = x2.shape[0]
    tok = pl.BlockSpec((FFN_TM, D_MODEL), lambda i: (i, 0))
    return pl.pallas_call(
        _ffn_kernel,
        grid=(n // FFN_TM,),
        in_specs=[tok, _resident((1, D_MODEL)), _resident((D_MODEL, 2 * D_FF)), _resident((D_FF, D_MODEL))],
        out_specs=tok,
        out_shape=jax.ShapeDtypeStruct((n, D_MODEL), jnp.float32),
        compiler_params=_params("parallel"),
        name="ffn",
    )(x2, g, w_gate_up, w_down)


def _mix_ffn_kernel(x_ref, ys_ref, ym_ref, yr_ref, wo_ref, g_ref, wgu_ref, wd_ref, fg_ref, o_ref, *, final_norm):
    o_ml, o_rw = SSD_WIDTH, SSD_WIDTH + MLSTM_WIDTH
    x = (x_ref[...] + _dot(ys_ref[...], wo_ref[0:o_ml, :]) + _dot(ym_ref[...], wo_ref[o_ml:o_rw, :])
         + _dot(yr_ref[...], wo_ref[o_rw:o_rw + RWKV_WIDTH, :]))
    y = _ffn_half_step(x, g_ref, wgu_ref, wd_ref)
    if final_norm:
        y = _rms(y, fg_ref[...])
    o_ref[...] = y


def _mix_ffn(x2, y_ssd, y_ml, y_rw, w_out, g, w_gate_up, w_down, final_g, final_norm):
    n = x2.shape[0]
    tok = lambda w: pl.BlockSpec((FFN_TM, w), lambda i: (i, 0))
    return pl.pallas_call(
        functools.partial(_mix_ffn_kernel, final_norm=final_norm),
        grid=(n // FFN_TM,),
        in_specs=[tok(D_MODEL), tok(SSD_WIDTH), tok(MLSTM_WIDTH), tok(RWKV_WIDTH),
                  _resident((SSD_WIDTH + MLSTM_WIDTH + RWKV_WIDTH, D_MODEL)), _resident((1, D_MODEL)),
                  _resident((D_MODEL, 2 * D_FF)), _resident((D_FF, D_MODEL)), _resident((1, D_MODEL))],
        out_specs=tok(D_MODEL),
        out_shape=jax.ShapeDtypeStruct((n, D_MODEL), jnp.float32),
        compiler_params=_params("parallel"),
        name="mix_ffn",
    )(x2, y_ssd, y_ml, y_rw, w_out, g, w_gate_up, w_down, final_g)


def _causal_conv(ext_ref, cur, w_ref, b_ref, first):
    rows = cur.shape[0]

    @pl.when(first)
    def _():
        ext_ref[0:HALO, :] = jnp.zeros((HALO, cur.shape[1]), jnp.float32)

    ext_ref[HALO:HALO + rows, :] = cur
    acc = b_ref[...] + cur * w_ref[CONV_K - 1:CONV_K, :]
    for j in range(CONV_K - 1):
        acc = acc + ext_ref[pl.ds(HALO - (CONV_K - 1) + j, rows), :] * w_ref[j:j + 1, :]
    ext_ref[0:HALO, :] = cur[rows - HALO:rows, :]
    return acc


def _split_terms(x, terms):
    parts, rest = [], x
    for i in range(terms):
        p = _bf(rest)
        parts.append(p)
        if i + 1 < terms:
            rest = rest - p.astype(jnp.float32)
    return parts


def _split_lanes(x, terms):
    return jnp.concatenate(_split_terms(x, terms), axis=1)


def _split_rows(x, terms):
    return jnp.concatenate(_split_terms(x, terms), axis=0)


def _ssd_steps(first, u_ref, sm_ref, cw_ref, cb_ref, dtb_ref, nega_ref, dskip_ref, ng_ref, tril_ref, expand_ref,
               selcol_ref, y_ref, ext_ref, st_ref):
    L, P, N, G = SSD_CHUNK, SSD_HEAD_DIM, SSD_STATE, SSD_GROUPS
    H = SSD_HEADS

    @pl.when(first)
    def _():
        st_ref[...] = jnp.zeros_like(st_ref)

    z = u_ref[:, 0:SSD_WIDTH]
    xbc = _silu(_causal_conv(ext_ref, u_ref[:, SSD_WIDTH:U_SSD_W], cw_ref, cb_ref, first))
    yield
    xs = xbc[:, 0:SSD_WIDTH]
    b_in = xbc[:, SSD_WIDTH:SSD_WIDTH + G * N]
    c_in = xbc[:, SSD_WIDTH + G * N:SSD_WIDTH + 2 * G * N]
    dt = _softplus(sm_ref[...] + dtb_ref[...])
    a = dt * nega_ref[...]
    a_cum = _dot(tril_ref[...], _split_rows(a, 3))
    a_cum_t = a_cum.T
    a_cum2 = _split_lanes(a_cum, 2)
    a_wide = _dot(a_cum2, expand_ref[...])
    col_b = _dot(a_cum2, selcol_ref[...])
    dt_wide = _dot(_split_lanes(dt, 2), expand_ref[...])
    exp_a = jnp.exp(a_wide)
    x_dt = xs * dt_wide
    xdd = x_dt * jnp.exp(a_wide[L - 1:L, :] - a_wide)

    yield
    row = lax.broadcasted_iota(jnp.int32, (L, L), 0)
    col = lax.broadcasted_iota(jnp.int32, (L, L), 1)
    causal = row >= col
    group0 = col < N
    c_b = _bf(c_in)
    b_b = _bf(b_in)
    cbs = [_dot_nt(_bf(jnp.where(group0 if g == 0 else ~group0, c_in, 0.0)), b_b) for g in range(G)]
    st = st_ref[...]
    y_off = _dot(c_b, _bf(st))

    yield
    x_dt_b = _bf(x_dt)
    y_parts = []
    for hp in range(H // 2):
        cb_g = cbs[(2 * hp) // (H // G)]
        x_pair = x_dt_b[:, hp * 2 * P:(hp + 1) * 2 * P]
        halves = []
        for h in (2 * hp, 2 * hp + 1):
            seg = col_b[:, h * LANES:(h + 1) * LANES] - a_cum_t[h:h + 1, :]
            halves.append(_dot(_bf(cb_g * jnp.exp(jnp.where(causal, seg, -jnp.inf))), x_pair))
        y_parts.append(jnp.where(group0, halves[0], halves[1]))
        yield
    y = jnp.concatenate(y_parts, axis=1) + y_off * exp_a + xs * dskip_ref[...]

    srow = lax.broadcasted_iota(jnp.int32, (G * N, H * P), 0)
    scol = lax.broadcasted_iota(jnp.int32, (G * N, H * P), 1)
    own = (srow >= N) == (scol >= (H // G) * P)
    st_ref[...] = st * exp_a[L - 1:L, :] + jnp.where(own, _dot_tn(b_b, _bf(xdd)), 0.0)

    yield
    y = y * _silu(z)
    gw = SSD_WIDTH // G
    y_ref[...] = _bf(jnp.concatenate(
        [_rms(y[:, g * gw:(g + 1) * gw], ng_ref[:, g * gw:(g + 1) * gw]) for g in range(G)], axis=1))


def _mlstm_steps(first, u_ref, sm_ref, cw_ref, cb_ref, wqk_ref, wv_ref, gb_ref, ng_ref, tril_ref, selcol_ref,
                 y_ref, ext_ref, cn_ref, m_ref):
    L, H, Dh = MLSTM_CHUNK, MLSTM_HEADS, MLSTM_HEAD_DIM

    @pl.when(first)
    def _():
        cn_ref[...] = jnp.zeros_like(cn_ref)
        m_ref[...] = jnp.zeros_like(m_ref)

    xm = u_ref[:, 0:MLSTM_WIDTH]
    o_pre = u_ref[:, MLSTM_WIDTH:U_ML_W]
    xc_b = _bf(_silu(_causal_conv(ext_ref, xm, cw_ref, cb_ref, first)))
    xm_b = _bf(xm)
    yield
    gates = sm_ref[...] + gb_ref[...]
    log_f = _log_sigmoid(gates)
    bcum = _dot(tril_ref[...], _split_rows(log_f, 3))
    bcum_t = bcum.T
    gates_t = gates.T
    col_f = _dot(_split_lanes(bcum, 2), selcol_ref[:, 0:H * LANES])
    col_i = _dot(_split_lanes(gates, 2), selcol_ref[:, H * LANES:2 * H * LANES])
    row = lax.broadcasted_iota(jnp.int32, (L, L), 0)
    col = lax.broadcasted_iota(jnp.int32, (L, L), 1)
    causal = row >= col
    ones = jnp.ones((L, Dh), jnp.bfloat16)
    twice = lambda x: jnp.concatenate([x, x], axis=1)
    heads = range(H)
    sls = [slice(h * Dh, (h + 1) * Dh) for h in heads]

    yield
    qk_proj = [_dot(xc_b[:, sls[h]], wqk_ref[h]) for h in heads]
    q_b = [_bf(p[:, 0:Dh]) for p in qk_proj]
    k = [p[:, Dh:2 * Dh] * (Dh ** -0.5) for p in qk_proj]
    v1 = [jnp.concatenate([_bf(_dot(xm_b[:, sls[h]], wv_ref[h])), ones], axis=1) for h in heads]
    b_col = [col_f[:, h * LANES:(h + 1) * LANES] for h in heads]
    i_col = [col_i[:, h * LANES:(h + 1) * LANES] for h in heads]
    b_end = [b[L - 1:L, :] for b in b_col]
    m_st = [m_ref[h:h + 1, :] for h in heads]
    cn_st = [cn_ref[h] for h in heads]

    yield
    d_log = [jnp.where(causal, b_col[h] - bcum_t[SM_F + h:SM_F + h + 1, :] + gates_t[SM_I + h:SM_I + h + 1, :],
                       -jnp.inf) for h in heads]
    inter_log = [b_col[h] + m_st[h] for h in heads]
    m_t = [jnp.maximum(inter_log[h], jnp.max(d_log[h], axis=1, keepdims=True)) for h in heads]
    qk = [_bf(_dot_nt(q_b[h], _bf(k[h])) * jnp.exp(d_log[h] - m_t[h])) for h in heads]
    inter = [_dot(q_b[h], _bf(cn_st[h])) for h in heads]
    yield
    outs = []
    for h in heads:
        nd = _dot(qk[h], v1[h]) + inter[h] * twice(jnp.exp(inter_log[h] - m_t[h]))
        hh = nd[:, 0:Dh] / jnp.maximum(jnp.abs(nd[:, Dh:2 * Dh]), jnp.exp(-m_t[h]))
        hh = hh * jax.nn.sigmoid(o_pre[:, sls[h]])
        outs.append(_rms(hh, ng_ref[:, sls[h]]))
    y_ref[...] = _bf(jnp.concatenate(outs, axis=1))
    yield

    for h in heads:
        end_log = b_end[h] - b_col[h] + i_col[h]
        m_loc = jnp.max(end_log, axis=0, keepdims=True)
        kw = k[h] * jnp.exp(end_log - m_loc)
        cn_loc = _dot_tn(_bf(kw), v1[h])
        m_new = jnp.maximum(b_end[h] + m_st[h], m_loc)
        s_old = jnp.exp(b_end[h] + m_st[h] - m_new)
        s_new = jnp.exp(m_loc - m_new)
        cn_ref[h] = cn_st[h] * twice(s_old) + cn_loc * twice(s_new)
        m_ref[h:h + 1, :] = m_new


def _unit_lower_inverse(a_list, idx_r, idx_c, n):
    size = a_list[0].shape[0]
    eye = (idx_r == idx_c).astype(jnp.float32)
    base = (idx_r >> 1) == (idx_c >> 1)
    ts = [eye + jnp.where(base, a, 0.0) for a in a_list]
    log_b = 1
    while (1 << log_b) < n:
        b = 1 << log_b
        join = ((idx_r >> (log_b + 1)) == (idx_c >> (log_b + 1))) & ((idx_r >> log_b) != (idx_c >> log_b))
        es = [_bf(jnp.where(join, a, 0.0)) for a in a_list]
        tbs = [_bf(t) for t in ts]
        if b < 8:
            tes = [_bf(_dot(tb, e)) for tb, e in zip(tbs, es)]
            ts = [t + _dot(te, tb) for t, te, tb in zip(ts, tes, tbs)]
        else:
            lower = [slice(s + b, s + 2 * b) for s in range(0, size, 2 * b)]
            t_low = [jnp.concatenate([t[sl] for sl in lower], axis=0) for t in ts]
            tes = [_bf(_dot(_bf(tl), e)) for tl, e in zip(t_low, es)]
            upd = [tl + _dot(te, tb) for tl, te, tb in zip(t_low, tes, tbs)]
            ts = [jnp.concatenate([piece for j, sl in enumerate(lower)
                                   for piece in (t[sl.start - b:sl.start], u[j * b:(j + 1) * b])], axis=0)
                  for t, u in zip(ts, upd)]
        log_b += 1
        yield
    return ts


def _head_sum(x, hsum_bf):
    rows = x.shape[0]
    s = _dot(_split_rows(x, 2), hsum_bf)
    return s[0:rows] + s[rows:2 * rows]


def _rwkv_steps(first, u_ref, mu_ref, w0_ref, wup_ref, a0_ref, aup_ref, gup_ref, kk_ref, ka_ref, rk_ref,
                lnw_ref, lnb_ref, tril_ref, hsum_ref, y_ref, ext_ref, s_ref):
    C, H, N, W, SUB = RWKV_CHUNK, RWKV_HEADS, RWKV_HEAD_DIM, RWKV_WIDTH, RWKV_SUB
    nb = u_ref.shape[0]
    rows = SUB * C
    nseg = nb * SUB

    @pl.when(first)
    def _():
        ext_ref[:, 0:HALO, :] = jnp.zeros((nb, HALO, U_RW_W), jnp.float32)
        s_ref[...] = jnp.zeros_like(s_ref)

    u_rows, prev_rows = [], []
    for b in range(nb):
        ub = u_ref[b]
        ext_ref[b, HALO:HALO + rows, :] = ub
        prev_rows.append(ext_ref[b, pl.ds(HALO - 1, rows), :])
        ext_ref[b, 0:HALO, :] = ub[rows - HALO:rows, :]
        u_rows.append(ub)
    u = jnp.concatenate(u_rows, axis=0)
    u_prev = jnp.concatenate(prev_rows, axis=0)
    yield
    us = u + (u_prev - u) * mu_ref[...]
    r = us[:, 0:W]
    k = us[:, W:2 * W]
    v = us[:, 2 * W:3 * W]
    wa_l = us[:, 3 * W:3 * W + RWKV_LORA_W]
    g_l = us[:, 3 * W + RWKV_LORA_W:3 * W + 2 * RWKV_LORA_W]
    w_log = -_softplus(-(w0_ref[...] + _dot(_bf(jnp.tanh(wa_l)), wup_ref[...]))) - 0.5
    ld = -jnp.exp(w_log)
    a = jax.nn.sigmoid(a0_ref[...] + _dot(_bf(wa_l), aup_ref[...]))
    g = _dot(_bf(jax.nn.sigmoid(g_l)), gup_ref[...])
    yield
    hsum = hsum_ref[...]
    kk = k * kk_ref[...]
    kk = kk / jnp.maximum(jnp.sqrt(_head_sum(kk * kk, hsum)), 1e-6)
    k = k * (1.0 + (a - 1.0) * ka_ref[...])

    yield
    cl = _dot(tril_ref[...], _split_rows(ld, 3))
    cl_end = jnp.concatenate([jnp.broadcast_to(cl[(s + 1) * C - 1:(s + 1) * C, :], (C, W)) for s in range(nseg)],
                             axis=0)
    p_inv = jnp.exp(-cl)
    p_end = jnp.exp(cl_end - cl)
    a_t = -kk * jnp.exp(cl - ld)
    kka = kk * a
    b_t = kka * p_inv
    k_t = k * p_inv
    r_t = r * jnp.exp(cl)
    b_e = kka * p_end
    k_e = k * p_end

    yield
    PW = 2 * N
    idx_r = lax.broadcasted_iota(jnp.int32, (PW, PW), 0)
    idx_c = lax.broadcasted_iota(jnp.int32, (PW, PW), 1)
    same_head = (idx_r >> 6) == (idx_c >> 6)
    strict = same_head & (idx_r > idx_c)
    incl = same_head & (idx_r >= idx_c)
    lane = lax.broadcasted_iota(jnp.int32, (C, PW), 1)
    head0 = lane < N

    def stack(x):
        return _bf(jnp.concatenate([jnp.where(head0, x, 0.0), jnp.where(head0, 0.0, x)], axis=0))

    npair = H // 2
    probs = [(s, p) for s in range(nseg) for p in range(npair)]
    cut = lambda x, s, p: x[s * C:(s + 1) * C, p * PW:(p + 1) * PW]
    at_s = [stack(cut(a_t, s, p)) for s, p in probs]
    rt_s = [stack(cut(r_t, s, p)) for s, p in probs]
    v_s = [stack(cut(v, s, p)) for s, p in probs]
    yield
    a_ab, a_ak, a_rbk = [], [], []
    for q, (s, p) in enumerate(probs):
        bt, kt = _bf(cut(b_t, s, p)), _bf(cut(k_t, s, p))
        aa = _dot_nt(jnp.concatenate([at_s[q], rt_s[q]], axis=0), jnp.concatenate([bt, bt, kt, kt], axis=0))
        a_ab.append(jnp.where(strict, aa[0:PW, 0:PW], 0.0))
        a_ak.append(_bf(jnp.where(strict, aa[0:PW, PW:2 * PW], 0.0)))
        a_rbk.append(jnp.concatenate([_bf(jnp.where(incl, aa[PW:2 * PW, 0:PW], 0.0)),
                                      _bf(jnp.where(incl, aa[PW:2 * PW, PW:2 * PW], 0.0))], axis=1))
    yield
    t_inv = yield from _unit_lower_inverse(a_ab, idx_r, idx_c, C)
    t_inv = [_bf(t) for t in t_inv]

    sts = [s_ref[i] for i in range(nb * npair)]
    y_parts = [None] * len(probs)
    for j in range(SUB):
        qs = [((b * SUB + j) * npair + p, b * npair + p) for b in range(nb) for p in range(npair)]
        rhs = [_dot(jnp.concatenate([at_s[q], a_ak[q]], axis=1),
                    jnp.concatenate([_bf(sts[i]), v_s[q]], axis=0)) for q, i in qs]
        yield
        uu = [_bf(_dot(t_inv[q], _bf(r_))) for (q, i), r_ in zip(qs, rhs)]
        yield
        for (q, i), u_ in zip(qs, uu):
            yy = _dot(jnp.concatenate([rt_s[q], a_rbk[q]], axis=1),
                      jnp.concatenate([_bf(sts[i]), u_, v_s[q]], axis=0))
            y_parts[q] = yy[0:C] + yy[C:2 * C]
        yield
        for (q, i), u_ in zip(qs, uu):
            s, p = probs[q]
            ld_end_col = jnp.broadcast_to(cl[(s + 1) * C - 1:(s + 1) * C, p * PW:(p + 1) * PW], (PW, PW)).T
            sts[i] = sts[i] * jnp.exp(ld_end_col) + _dot_tn(
                jnp.concatenate([stack(cut(b_e, s, p)), stack(cut(k_e, s, p))], axis=0),
                jnp.concatenate([u_, v_s[q]], axis=0))
        yield
    for i, st in enumerate(sts):
        s_ref[i] = st

    y = jnp.concatenate([jnp.concatenate(y_parts[s * npair:(s + 1) * npair], axis=1) for s in range(nseg)], axis=0)
    mu = _head_sum(y, hsum) * (1.0 / N)
    yield
    yc = y - mu
    var = _head_sum(yc * yc, hsum) * (1.0 / N)
    yield
    y = yc * lax.rsqrt(var + RWKV_GN_EPS) * lnw_ref[...] + lnb_ref[...]
    y = y + _head_sum(r * k * rk_ref[...], hsum) * v
    y = y * g
    for b in range(nb):
        y_ref[b] = _bf(y[b * rows:(b + 1) * rows])


def _chain(generators):
    for gen in generators:
        yield from gen


def _interleave(streams):
    live = list(streams)
    while live:
        for item in list(live):
            gen, per_turn = item
            try:
                for _ in range(per_turn):
                    next(gen)
            except StopIteration:
                live.remove(item)


PROJ_COLS = 768


def _project_steps(x_ref, g_ref, w_refs, u_refs):
    nb, rows, _ = x_ref.shape
    h = _bf(_rms(x_ref[...].reshape(nb * rows, D_MODEL), g_ref[...]))
    yield
    for w_ref, u_ref in zip(w_refs, u_refs):
        width = w_ref.shape[1]
        for lo in range(0, width, PROJ_COLS):
            hi = min(lo + PROJ_COLS, width)
            part = _dot(h, w_ref[:, lo:hi])
            for b in range(nb):
                u_ref[b, :, lo:hi] = part[b * rows:(b + 1) * rows]
            yield


def _mixers_kernel(x_ref, xn_ref, p_g, p_wssd, p_wml, p_wrw, p_wsm,
                   s_cw, s_cb, s_dtb, s_nega, s_dskip, s_ng, tril_ref, s_expand, s_selcol,
                   m_cw, m_cb, m_wqk, m_wv, m_gb, m_ng, m_selcol,
                   r_mu, r_w0, r_wup, r_a0, r_aup, r_gup, r_kk, r_ka, r_rk, r_lnw, r_lnb, r_tril, r_hsum,
                   ys_ref, ym_ref, yr_ref,
                   us_buf, um_buf, ur_buf, sm_buf, s_ext, s_st, m_ext, m_cn, m_m, r_ext, r_s):
    step = pl.program_id(0)
    first = step == 0
    cur = step % 2
    nxt = 1 - cur
    nb = x_ref.shape[0]
    p_w = (p_wssd, p_wml, p_wrw, p_wsm)
    bufs = (us_buf, um_buf, ur_buf, sm_buf)

    @pl.when(first)
    def _():
        for _ in _project_steps(x_ref, p_g, p_w, [buf.at[0] for buf in bufs]):
            pass

    us_ref, um_ref, ur_ref, sm_ref = (buf.at[cur] for buf in bufs)
    project = _project_steps(xn_ref, p_g, p_w, [buf.at[nxt] for buf in bufs])
    rwkv = _rwkv_steps(first, ur_ref, r_mu, r_w0, r_wup, r_a0, r_aup, r_gup, r_kk, r_ka, r_rk, r_lnw, r_lnb,
                       r_tril, r_hsum, yr_ref, r_ext, r_s)
    vec = []
    for b in range(nb):
        vec.append(_ssd_steps(first, us_ref.at[b], sm_ref.at[b], s_cw, s_cb, s_dtb, s_nega, s_dskip, s_ng,
                              tril_ref, s_expand, s_selcol, ys_ref.at[b], s_ext.at[b], s_st.at[b]))
    for b in range(nb):
        vec.append(_mlstm_steps(first, um_ref.at[b], sm_ref.at[b], m_cw, m_cb, m_wqk, m_wv, m_gb, m_ng,
                                tril_ref, m_selcol, ym_ref.at[b], m_ext.at[b], m_cn.at[b], m_m.at[b]))
    _interleave([(rwkv, 1), (_chain(vec), 1), (project, 1)])


def _mixers(x3, proj_consts, ssd_consts, ml_consts, rw_consts, tril3):
    bsz, t, _ = x3.shape
    nblk = t // MIX_STEP
    tok = lambda w: pl.BlockSpec((bsz, MIX_STEP, w), lambda c: (0, c, 0))
    tok_first = pl.BlockSpec((bsz, MIX_STEP, D_MODEL), lambda c: (0, 0, 0))
    tok_next = pl.BlockSpec((bsz, MIX_STEP, D_MODEL), lambda c: (0, jnp.minimum(c + 1, nblk - 1), 0))
    whole = lambda arr: _resident(arr.shape)
    consts = (list(proj_consts) + list(ssd_consts[:6]) + [tril3] + list(ssd_consts[6:]) + list(ml_consts)
              + list(rw_consts))
    H, Dh = MLSTM_HEADS, MLSTM_HEAD_DIM
    u_buf = lambda w: pltpu.VMEM((2, bsz, MIX_STEP, w), jnp.float32)
    return pl.pallas_call(
        _mixers_kernel,
        grid=(nblk,),
        in_specs=[tok_first, tok_next] + [whole(c) for c in consts],
        out_specs=[tok(SSD_WIDTH), tok(MLSTM_WIDTH), tok(RWKV_WIDTH)],
        out_shape=[jax.ShapeDtypeStruct((bsz, t, w), jnp.bfloat16) for w in (SSD_WIDTH, MLSTM_WIDTH, RWKV_WIDTH)],
        scratch_shapes=[
            u_buf(U_SSD_W), u_buf(U_ML_W), u_buf(U_RW_W), u_buf(U_SM_W),
            pltpu.VMEM((bsz, HALO + MIX_STEP, D_XBC), jnp.float32),
            pltpu.VMEM((bsz, SSD_GROUPS * SSD_STATE, SSD_WIDTH), jnp.float32),
            pltpu.VMEM((bsz, HALO + MIX_STEP, MLSTM_WIDTH), jnp.float32),
            pltpu.VMEM((bsz, H, Dh, 2 * Dh), jnp.float32),
            pltpu.VMEM((bsz, SUBLANES, LANES), jnp.float32),
            pltpu.VMEM((bsz, HALO + MIX_STEP, U_RW_W), jnp.float32),
            pltpu.VMEM((bsz * RWKV_HEADS // 2, 2 * RWKV_HEAD_DIM, 2 * RWKV_HEAD_DIM), jnp.float32),
        ],
        compiler_params=_params("arbitrary"),
        name="mixers",
    )(x3, x3, *consts)


def _pad_lanes(vec, offset, width=U_SM_W):
    out = jnp.zeros((1, width), jnp.float32)
    return out.at[0, offset:offset + vec.shape[0]].set(vec)


def kernel(x, ffn1_norm, ffn1_w_gate_up, ffn1_w_down, mix_norm, w_in, ssd_conv_w, ssd_conv_b, ssd_dt_bias, ssd_a_log, ssd_d, ssd_norm, mlstm_conv_w, mlstm_conv_b, mlstm_wq, mlstm_wk, mlstm_wv, mlstm_gate_bias, mlstm_norm, rwkv_shift_mu, rwkv_w0, rwkv_w_up, rwkv_a0, rwkv_a_up, rwkv_g_up, rwkv_k_k, rwkv_k_a, rwkv_r_k, rwkv_ln_w, rwkv_ln_b, w_out, ffn2_norm, ffn2_w_gate_up, ffn2_w_down, final_norm):
    bsz, t, d = x.shape
    depth = w_in.shape[0]
    n = bsz * t
    f32 = jnp.float32
    row = lambda v: v.reshape(1, -1).astype(f32)

    tril_l = jnp.tril(jnp.ones((SSD_CHUNK, SSD_CHUNK), f32))
    tril3 = _bf(jnp.concatenate([tril_l] * 3, axis=1))
    lane_head = jnp.arange(SSD_WIDTH) // SSD_HEAD_DIM
    expand1 = (jnp.arange(U_SM_W)[:, None] == lane_head[None, :]).astype(f32)
    expand = _bf(jnp.concatenate([expand1] * 2, axis=0))
    tile_head = jnp.arange(SSD_HEADS * LANES) // LANES
    selcol1 = (jnp.arange(U_SM_W)[:, None] == tile_head[None, :]).astype(f32)
    selcol = _bf(jnp.concatenate([selcol1] * 2, axis=0))
    ml_src = jnp.concatenate([SM_F + jnp.arange(MLSTM_HEADS), SM_I + jnp.arange(MLSTM_HEADS)])
    ml_sel1 = (jnp.arange(U_SM_W)[:, None] == jnp.repeat(ml_src, LANES)[None, :]).astype(f32)
    ml_selcol = _bf(jnp.concatenate([ml_sel1] * 2, axis=0))
    tril_c = jnp.kron(jnp.eye(bsz * RWKV_SUB, dtype=f32), jnp.tril(jnp.ones((RWKV_CHUNK, RWKV_CHUNK), f32)))
    tril_c = _bf(jnp.concatenate([tril_c] * 3, axis=1))
    head_id = jnp.arange(RWKV_WIDTH) // RWKV_HEAD_DIM
    hsum = _bf(head_id[:, None] == head_id[None, :])
    final_g = row(final_norm)

    x2 = x.reshape(n, d)
    for l in range(depth):
        wl = _bf(w_in[l])
        o_ml = D_SSD_IN
        o_rw = D_SSD_IN + D_MLSTM_IN
        w_ssd = wl[:, 0:U_SSD_W]
        w_ml = wl[:, o_ml:o_ml + U_ML_W]
        w_rw = wl[:, o_rw:o_rw + U_RW_W]
        w_sm = jnp.concatenate(
            [wl[:, U_SSD_W:D_SSD_IN], wl[:, o_ml + U_ML_W:o_rw],
             jnp.zeros((d, U_SM_W - SSD_HEADS - 2 * MLSTM_HEADS), jnp.bfloat16)], axis=1)

        x2 = _ffn(x2, row(ffn1_norm[l]), _bf(ffn1_w_gate_up[l]), _bf(ffn1_w_down[l]))

        zeros64 = jnp.zeros((64, RWKV_WIDTH), f32)
        y_ssd, y_ml, y_rw = _mixers(
            x2.reshape(bsz, t, d), (row(mix_norm[l]), w_ssd, w_ml, w_rw, w_sm),
            (ssd_conv_w[l], row(ssd_conv_b[l]), _pad_lanes(ssd_dt_bias[l], SM_DT),
             _pad_lanes(-jnp.exp(ssd_a_log[l]), SM_DT), row(jnp.repeat(ssd_d[l], SSD_HEAD_DIM)), row(ssd_norm[l]),
             expand, selcol),
            (mlstm_conv_w[l], row(mlstm_conv_b[l]), _bf(jnp.concatenate([mlstm_wq[l], mlstm_wk[l]], axis=-1)),
             _bf(mlstm_wv[l]), _pad_lanes(mlstm_gate_bias[l], SM_I), row(mlstm_norm[l]), ml_selcol),
            (row(rwkv_shift_mu[l]), row(rwkv_w0[l]), _bf(jnp.concatenate([rwkv_w_up[l], zeros64], axis=0)),
             row(rwkv_a0[l]), _bf(jnp.concatenate([zeros64, rwkv_a_up[l]], axis=0)), _bf(rwkv_g_up[l]),
             row(rwkv_k_k[l]), row(rwkv_k_a[l]), row(rwkv_r_k[l]), row(rwkv_ln_w[l]), row(rwkv_ln_b[l]),
             tril_c, hsum),
            tril3)

        x2 = _mix_ffn(x2, y_ssd.reshape(n, SSD_WIDTH), y_ml.reshape(n, MLSTM_WIDTH), y_rw.reshape(n, RWKV_WIDTH),
                      _bf(w_out[l]),
                      row(ffn2_norm[l]), _bf(ffn2_w_gate_up[l]), _bf(ffn2_w_down[l]), final_g, l == depth - 1)
    return x2.reshape(bsz, t, d)
```
